```python
import jax, jax.numpy as jnp
from jax import lax
import numpy as np

D_MODEL = 2048
BATCH = 16
SEQ = 2048
DEPTH = 2

N_A_LAYERS = DEPTH // 2
N_B_LAYERS = DEPTH - N_A_LAYERS
POOL_WINDOWS = (2, 4, 8, 16)
N_POOL_GROUPS = len(POOL_WINDOWS)
POOL_GROUP_DIM = D_MODEL // N_POOL_GROUPS
HEAD_DIM = 128
N_HEADS = D_MODEL // HEAD_DIM
Q_BLOCK = 128
N_GROUPS = 4
EXPERTS_PER_GROUP = 4
N_EXPERTS = N_GROUPS * EXPERTS_PER_GROUP
TOP_K_IN_GROUP = 2
D_FF_EXPERT = D_MODEL // 4
EPS = 1e-6

kernel_name = "hybrid_pool_stickbreak_hmoe"


def rmsnorm(x, g):
    xf = x.astype(jnp.float32)
    y = xf * lax.rsqrt(jnp.mean(xf * xf, axis=-1, keepdims=True) + EPS)
    return (y * g.astype(jnp.float32)).astype(x.dtype)


def pool_mixer(x, pool_w, pool_scale):
    B, S, D = x.shape
    xf = x.reshape(B, S, N_POOL_GROUPS, POOL_GROUP_DIM).astype(jnp.float32)
    csum = jnp.cumsum(xf, axis=1)
    pos = jnp.arange(S)
    outs = []
    for gi, w in enumerate(POOL_WINDOWS):
        c = csum[:, :, gi]
        c_prev = jnp.pad(c, ((0, 0), (w, 0), (0, 0)))[:, :S]
        cnt = jnp.minimum(pos + 1, w).astype(jnp.float32)[None, :, None]
        outs.append((c - c_prev) / cnt - xf[:, :, gi])
    pooled = jnp.stack(outs, axis=2).astype(x.dtype)
    y = jnp.einsum("bsgc,gcd->bsgd", pooled, pool_w).reshape(B, S, D)
    return y * pool_scale


def head_rms(x, g):
    xf = x.astype(jnp.float32)
    y = xf * lax.rsqrt(jnp.mean(xf * xf, axis=-1, keepdims=True) + EPS)
    return (y * g.astype(jnp.float32)).astype(x.dtype)


def shared_kv(h, kv_norm, w_kv, k_norm):
    B, S, D = h.shape
    kv = rmsnorm(h, kv_norm) @ w_kv
    k, v = jnp.split(kv, 2, axis=-1)
    k = head_rms(k.reshape(B, S, N_HEADS, HEAD_DIM), k_norm)
    v = v.reshape(B, S, N_HEADS, HEAD_DIM)
    return k.transpose(0, 2, 1, 3), v.transpose(0, 2, 1, 3)


def stick_breaking(q, k, v):
    S = q.shape[2]
    scale = HEAD_DIM ** -0.5
    outs = []
    for b0 in range(0, S, Q_BLOCK):
        n_k = b0 + Q_BLOCK
        qb = q[:, :, b0:n_k]
        kb = k[:, :, :n_k]
        vb = v[:, :, :n_k]
        z = jnp.einsum("bhqd,bhkd->bhqk", qb, kb).astype(jnp.float32) * scale
        t_pos = b0 + jnp.arange(Q_BLOCK)
        s_pos = jnp.arange(n_k)
        causal = s_pos[None, :] < t_pos[:, None]
        log_keep = jnp.where(causal, jax.nn.log_sigmoid(-z), 0.0)
        after = lax.cumsum(log_keep, axis=3, reverse=True) - log_keep
        a = jnp.where(causal, jnp.exp(jax.nn.log_sigmoid(z) + after), 0.0)
        outs.append(jnp.einsum("bhqk,bhkd->bhqd", a.astype(vb.dtype), vb))
    return jnp.concatenate(outs, axis=2)


def stick_breaking_layer(x, k, v, w_q, q_norm, w_o):
    B, S, D = x.shape
    q = head_rms((x @ w_q).reshape(B, S, N_HEADS, HEAD_DIM), q_norm)
    o = stick_breaking(q.transpose(0, 2, 1, 3), k, v)
    return o.transpose(0, 2, 1, 3).reshape(B, S, D) @ w_o


def hier_moe(x, w_grp, b_grp, w_rt, b_rt, w_gate, w_up, w_down):
    B, S, D = x.shape
    T = B * S
    xt = x.reshape(T, D)
    grp_logits = (xt @ w_grp + b_grp).astype(jnp.float32)
    grp_prob = jax.nn.softmax(grp_logits, axis=-1)
    g_idx = jnp.argmax(grp_logits, axis=-1)
    g_w = jnp.take_along_axis(grp_prob, g_idx[:, None], axis=-1)
    exp_logits = (jnp.einsum("td,gde->tge", xt, w_rt) + b_rt).astype(jnp.float32)
    sel = jnp.take_along_axis(exp_logits, g_idx[:, None, None], axis=1)[:, 0]
    top_v, top_i = lax.top_k(sel, TOP_K_IN_GROUP)
    top_w = jax.nn.softmax(top_v, axis=-1) * g_w
    within = jnp.sum(jax.nn.one_hot(top_i, EXPERTS_PER_GROUP, dtype=jnp.float32)
                     * top_w[..., None], axis=1)
    gate = (jax.nn.one_hot(g_idx, N_GROUPS, dtype=jnp.float32)[:, :, None]
            * within[:, None, :]).reshape(T, N_EXPERTS).astype(x.dtype)
    h = jax.nn.silu(jnp.einsum("td,edf->tef", xt, w_gate)) * jnp.einsum("td,edf->tef", xt, w_up)
    y = jnp.einsum("tef,efd->td", h * gate[:, :, None], w_down)
    return y.reshape(B, S, D)


def setup_inputs(seed: int = 0) -> dict:
    key = jax.random.key(seed)
    ks = jax.random.split(key, 20)
    f32 = jnp.float32
    D, C, HD, F = D_MODEL, POOL_GROUP_DIM, HEAD_DIM, D_FF_EXPERT

    def nrm(k, shape, fan_in):
        return jax.random.normal(k, shape, f32) * fan_in ** -0.5

    def gain(k, shape):
        return 1.0 + 0.05 * jax.random.normal(k, shape, f32)

    return {
        "x": jax.random.normal(ks[0], (BATCH, SEQ, D), f32),
        "norm_mix": gain(ks[1], (DEPTH, D)),
        "norm_ffn": gain(ks[2], (DEPTH, D)),
        "pool_w": nrm(ks[3], (N_A_LAYERS, N_POOL_GROUPS, C, C), C),
        "pool_scale": gain(ks[4], (N_A_LAYERS, D)),
        "kv_norm": gain(ks[5], (D,)),
        "w_kv": nrm(ks[6], (D, 2 * D), D),
        "k_norm": gain(ks[7], (HD,)),
        "w_q": nrm(ks[8], (N_B_LAYERS, D, D), D),
        "q_norm": gain(ks[9], (N_B_LAYERS, HD)),
        "w_o": nrm(ks[10], (N_B_LAYERS, D, D), D),
        "moe_w_grp": nrm(ks[11], (DEPTH, D, N_GROUPS), D),
        "moe_b_grp": 0.01 * jax.random.normal(ks[12], (DEPTH, N_GROUPS), f32),
        "moe_w_rt": nrm(ks[13], (DEPTH, N_GROUPS, D, EXPERTS_PER_GROUP), D),
        "moe_b_rt": 0.01 * jax.random.normal(ks[14], (DEPTH, N_GROUPS, EXPERTS_PER_GROUP), f32),
        "moe_w_gate": nrm(ks[15], (DEPTH, N_EXPERTS, D, F), D),
        "moe_w_up": nrm(ks[16], (DEPTH, N_EXPERTS, D, F), D),
        "moe_w_down": nrm(ks[17], (DEPTH, N_EXPERTS, F, D), F),
    }


def reference(x, norm_mix, norm_ffn, pool_w, pool_scale, kv_norm, w_kv, k_norm,
              w_q, q_norm, w_o, moe_w_grp, moe_b_grp, moe_w_rt, moe_b_rt,
              moe_w_gate, moe_w_up, moe_w_down):
    h = x
    k = v = None
    for i in range(DEPTH):
        if i < N_A_LAYERS:
            h = h + pool_mixer(rmsnorm(h, norm_mix[i]), pool_w[i], pool_scale[i])
        else:
            j = i - N_A_LAYERS
            if j == 0:
                k, v = shared_kv(h, kv_norm, w_kv, k_norm)
            h = h + stick_breaking_layer(rmsnorm(h, norm_mix[i]), k, v,
                                         w_q[j], q_norm[j], w_o[j])
        h = h + hier_moe(rmsnorm(h, norm_ffn[i]), moe_w_grp[i], moe_b_grp[i],
                         moe_w_rt[i], moe_b_rt[i], moe_w_gate[i], moe_w_up[i],
                         moe_w_down[i])
    return h
```

```python
import functools

import jax
import jax.numpy as jnp
from jax import lax
from jax.experimental import pallas as pl
from jax.experimental.pallas import tpu as pltpu

F32 = jnp.float32
BF16 = jnp.bfloat16

EPS = 1e-6
POOL_WINDOWS = (2, 4, 8, 16)
POOL_HALO = 16
HEAD_DIM = 128
LANES = 128
SUBLANES = 8
LOG2E = 1.4426950408889634

TOKEN_TILE = 256
MOE_TILE = 256
PROJ_TILE = 512
PROJ_COLS = 512
ATTN_TILE = 256
CLASS_ROWS = 32
VMEM_LIMIT = 56 * 1024 * 1024


def _params(*sem):
    return pltpu.CompilerParams(dimension_semantics=sem, vmem_limit_bytes=VMEM_LIMIT)


def _rms_scale(x):
    return lax.rsqrt(jnp.mean(x * x, axis=-1, keepdims=True) + EPS)


def _first_argmax(vals):
    best, idx = vals[0], jnp.zeros(vals[0].shape, jnp.int32)
    for i in range(1, len(vals)):
        upd = vals[i] > best
        best = jnp.where(upd, vals[i], best)
        idx = jnp.where(upd, i, idx)
    return best, idx


def _ffn_norm_route(h, gffn_ref, wr_ref, br_ref, xn_ref, rt_ref, cls_ref, n_grp, n_exp):
    ts = h.shape[0]
    xnb = (h * _rms_scale(h) * gffn_ref[...]).astype(BF16)
    xn_ref[...] = xnb
    logits = jnp.dot(xnb, wr_ref[...], preferred_element_type=F32) + br_ref[...]
    lt = logits.T
    row = lambda i: lt[i:i + 1, :]
    g_best, g_idx = _first_argmax([row(g) for g in range(n_grp)])
    denom = sum(jnp.exp(row(g) - g_best) for g in range(n_grp))
    g_w = 1.0 / denom
    sel = []
    for e in range(n_exp):
        v = row(n_grp + e)
        for g in range(1, n_grp):
            v = jnp.where(g_idx == g, row(n_grp + g * n_exp + e), v)
        sel.append(v)
    v1, i1 = _first_argmax(sel)
    v2 = jnp.full_like(v1, -jnp.inf)
    i2 = jnp.zeros_like(i1)
    for e in range(n_exp):
        upd = (i1 != e) & (sel[e] > v2)
        v2 = jnp.where(upd, sel[e], v2)
        i2 = jnp.where(upd, e, i2)
    t = jnp.exp(v2 - v1)
    w1 = g_w / (1.0 + t)
    w2 = w1 * t
    first_low = i1 < i2
    lo = jnp.where(first_low, i1, i2)
    hi = jnp.where(first_low, i2, i1)
    w_lo = jnp.where(first_low, w1, w2)
    w_hi = jnp.where(first_low, w2, w1)
    n_pairs = n_exp * (n_exp - 1) // 2
    pair = ((lo * (2 * n_exp - 1 - lo)) >> 1) + (hi - lo - 1)
    cls = (g_idx * n_pairs + pair).astype(F32)
    r = lax.broadcasted_iota(jnp.int32, (LANES, ts), 0)
    packed = jnp.where(r == 0, cls, jnp.where(r == 1, w_lo, jnp.where(r == 2, w_hi, 0.0)))
    cls_ref[0] = packed[:SUBLANES, :]
    rt_ref[...] = packed.T


def _pool_route_kernel(x_ref, gmix_ref, pw_ref, pscale_ref, gffn_ref, wr_ref, br_ref,
                       h_ref, xn_ref, rt_ref, cls_ref, halo_ref, *, n_grp, n_exp):
    s = pl.program_id(1)
    ts, d = x_ref.shape
    c = d // len(POOL_WINDOWS)
    x = x_ref[...]
    xn = x * _rms_scale(x) * gmix_ref[...]

    @pl.when(s == 0)
    def _():
        halo_ref[...] = jnp.zeros_like(halo_ref)

    ext = jnp.concatenate([halo_ref[...], xn], axis=0)
    halo_ref[...] = xn[ts - POOL_HALO:, :]
    pos = s * ts + lax.broadcasted_iota(jnp.int32, (ts, LANES), 0)
    for gi, w in enumerate(POOL_WINDOWS):
        acc = ext[:, gi * c:(gi + 1) * c]
        k = 1
        while k < w:
            acc = acc + pltpu.roll(acc, k, 0)
            k *= 2
        inv_cnt = 1.0 / jnp.minimum(pos + 1, w).astype(F32)
        inv_cnt = jnp.concatenate([inv_cnt] * (c // LANES), axis=1)
        pooled = acc[POOL_HALO:, :] * inv_cnt - xn[:, gi * c:(gi + 1) * c]
        y = jnp.dot(pooled.astype(BF16), pw_ref[gi], preferred_element_type=F32)
        h_ref[:, gi * c:(gi + 1) * c] = (x[:, gi * c:(gi + 1) * c]
                                         + y * pscale_ref[:, gi * c:(gi + 1) * c])
    _ffn_norm_route(h_ref[...], gffn_ref, wr_ref, br_ref, xn_ref, rt_ref, cls_ref, n_grp, n_exp)


def _route_out(t, d, ts):
    shapes = (jax.ShapeDtypeStruct((t, d), F32),
              jax.ShapeDtypeStruct((t, d), BF16),
              jax.ShapeDtypeStruct((t, LANES), F32),
              jax.ShapeDtypeStruct((t // ts, SUBLANES, ts), F32))
    return shapes


def _pool_route(x2, gmix, pw, pscale, gffn, wr, br, batch, n_grp, n_exp):
    t, d = x2.shape
    ts = TOKEN_TILE
    seq_tiles = t // batch // ts
    tok = lambda b, s: (b * seq_tiles + s, 0)
    fixed2 = lambda b, s: (0, 0)
    return pl.pallas_call(
        functools.partial(_pool_route_kernel, n_grp=n_grp, n_exp=n_exp),
        grid=(batch, seq_tiles),
        in_specs=[pl.BlockSpec((ts, d), tok),
                  pl.BlockSpec((1, d), fixed2),
                  pl.BlockSpec(pw.shape, lambda b, s: (0, 0, 0)),
                  pl.BlockSpec((1, d), fixed2),
                  pl.BlockSpec((1, d), fixed2),
                  pl.BlockSpec((d, LANES), fixed2),
                  pl.BlockSpec((1, LANES), fixed2)],
        out_specs=(pl.BlockSpec((ts, d), tok),
                   pl.BlockSpec((ts, d), tok),
                   pl.BlockSpec((ts, LANES), tok),
                   pl.BlockSpec((1, SUBLANES, ts), lambda b, s: (b * seq_tiles + s, 0, 0))),
        out_shape=_route_out(t, d, ts),
        scratch_shapes=[pltpu.VMEM((POOL_HALO, d), F32)],
        compiler_params=_params("arbitrary", "arbitrary"),
        name="pool_route",
    )(x2, gmix, pw, pscale, gffn, wr, br)


def _positions_kernel(cls_ref, pos_ref, tiles_ref, *, tm):
    n_t, _, ts = cls_ref.shape
    ci = lax.broadcasted_iota(jnp.int32, (CLASS_ROWS, ts), 0).astype(F32)

    def onehot(i):
        return (cls_ref[i, 0:1, :] == ci).astype(F32)

    acc = lax.fori_loop(0, n_t, lambda i, a: a + onehot(i), jnp.zeros((CLASS_ROWS, ts), F32))
    counts = jnp.sum(acc, axis=1, keepdims=True)
    tiles = jnp.floor((counts + (tm - 1)) * (1.0 / tm))
    tiles_b = jnp.broadcast_to(tiles, (CLASS_ROWS, LANES))
    tiles_ref[...] = tiles_b
    rr = lax.broadcasted_iota(jnp.int32, (CLASS_ROWS, CLASS_ROWS), 0)
    cc = lax.broadcasted_iota(jnp.int32, (CLASS_ROWS, CLASS_ROWS), 1)
    before = (cc < rr).astype(BF16)
    start = jnp.dot(before, tiles_b.astype(BF16), preferred_element_type=F32)[:, 0:1] * tm
    jj = lax.broadcasted_iota(jnp.int32, (ts, ts), 0)
    ss = lax.broadcasted_iota(jnp.int32, (ts, ts), 1)
    earlier = (jj < ss).astype(BF16)

    def body(i, run):
        oh = onehot(i)
        rank = jnp.dot(oh.astype(BF16), earlier, preferred_element_type=F32)
        pos_ref[i] = jnp.sum(oh * (run + rank), axis=0, keepdims=True).astype(jnp.int32)
        return run + jnp.sum(oh, axis=1, keepdims=True)

    lax.fori_loop(0, n_t, body, start)


def _positions(cls3, tm):
    n_t, _, ts = cls3.shape
    return pl.pallas_call(
        functools.partial(_positions_kernel, tm=tm),
        out_shape=(jax.ShapeDtypeStruct((n_t, 1, ts), jnp.int32),
                   jax.ShapeDtypeStruct((CLASS_ROWS, LANES), F32)),
        compiler_params=pltpu.CompilerParams(vmem_limit_bytes=VMEM_LIMIT),
        name="positions",
    )(cls3)


def _row_copies(n, make):
    def start(r, _):
        make(r).start()
        return 0

    def wait(r, _):
        make(r).wait()
        return 0

    lax.fori_loop(0, n, start, 0)
    lax.fori_loop(0, n, wait, 0)


def _dispatch_kernel(tail_ref, pos_ref, xn_ref, rt_ref, xs_hbm, rows_ref, zero_ref, sem, *, tm):
    td, d = xn_ref.shape

    @pl.when(pl.program_id(0) == 0)
    def _():
        zero_ref[...] = jnp.zeros_like(zero_ref)

        def zero_copy(c):
            start = pl.multiple_of(tail_ref[c], tm)
            return pltpu.make_async_copy(zero_ref, xs_hbm.at[pl.ds(start, tm)], sem)

        for c in range(tail_ref.shape[0]):
            @pl.when(tail_ref[c] >= 0)
            def _():
                zero_copy(c).start()
        for c in range(tail_ref.shape[0]):
            @pl.when(tail_ref[c] >= 0)
            def _():
                zero_copy(c).wait()

    rows_ref[:, :d] = xn_ref[...].astype(F32)
    rows_ref[:, d:] = rt_ref[...]
    _row_copies(td, lambda r: pltpu.make_async_copy(
        rows_ref.at[pl.ds(r, 1)], xs_hbm.at[pl.ds(pos_ref[0, 0, r], 1)], sem))


def _dispatch(tail, pos3, xn, rt, n_rows, tm):
    t, d = xn.shape
    td = pos3.shape[2]
    dw = d + LANES
    return pl.pallas_call(
        functools.partial(_dispatch_kernel, tm=tm),
        grid_spec=pltpu.PrefetchScalarGridSpec(
            num_scalar_prefetch=1,
            grid=(t // td,),
            in_specs=[pl.BlockSpec((1, 1, td), lambda i, tail: (i, 0, 0), memory_space=pltpu.SMEM),
                      pl.BlockSpec((td, d), lambda i, tail: (i, 0)),
                      pl.BlockSpec((td, LANES), lambda i, tail: (i, 0))],
            out_specs=pl.BlockSpec(memory_space=pl.ANY),
            scratch_shapes=[pltpu.VMEM((td, dw), F32),
                            pltpu.VMEM((tm, dw), F32),
                            pltpu.SemaphoreType.DMA(())]),
        out_shape=jax.ShapeDtypeStruct((n_rows, dw), F32),
        compiler_params=_params("arbitrary"),
        name="dispatch",
    )(tail, pos3, xn, rt)


def _moe_kernel(elo_ref, ehi_ref, nact_ref, xs_ref, wg_lo, wg_hi, wu_lo, wu_hi, wd_lo, wd_hi, ys_ref):
    d = ys_ref.shape[1]

    @pl.when(pl.program_id(0) < nact_ref[0])
    def _():
        x = xs_ref[:, :d].astype(BF16)

        def expert(wg, wu, lane):
            g = jnp.dot(x, wg[0], preferred_element_type=F32)
            u = jnp.dot(x, wu[0], preferred_element_type=F32)
            gate = xs_ref[:, d + lane:d + lane + 1]
            return (g * (1.0 / (1.0 + jnp.exp(-g))) * u * gate).astype(BF16)

        ys_ref[...] = (jnp.dot(expert(wg_lo, wu_lo, 1), wd_lo[0], preferred_element_type=F32)
                       + jnp.dot(expert(wg_hi, wu_hi, 2), wd_hi[0], preferred_element_type=F32))


def _moe(e_lo, e_hi, n_act, xs, wg, wu, wd, tm):
    n_rows, dw = xs.shape
    d = dw - LANES
    f = wg.shape[2]
    n_tiles = n_rows // tm
    tile = lambda i, lo, hi, na: (jnp.minimum(i, na[0] - 1), 0)
    w_lo = lambda i, lo, hi, na: (lo[i], 0, 0)
    w_hi = lambda i, lo, hi, na: (hi[i], 0, 0)
    return pl.pallas_call(
        _moe_kernel,
        grid_spec=pltpu.PrefetchScalarGridSpec(
            num_scalar_prefetch=3,
            grid=(n_tiles,),
            in_specs=[pl.BlockSpec((tm, dw), tile),
                      pl.BlockSpec((1, d, f), w_lo), pl.BlockSpec((1, d, f), w_hi),
                      pl.BlockSpec((1, d, f), w_lo), pl.BlockSpec((1, d, f), w_hi),
                      pl.BlockSpec((1, f, d), w_lo), pl.BlockSpec((1, f, d), w_hi)],
            out_specs=pl.BlockSpec((tm, d), tile)),
        out_shape=jax.ShapeDtypeStruct((n_rows, d), F32),
        compiler_params=_params("arbitrary"),
        name="moe",
    )(e_lo, e_hi, n_act, xs, wg, wg, wu, wu, wd, wd)


def _combine_kernel(pos_ref, h_ref, ys_hbm, out_ref, rows_ref, sem):
    tc = h_ref.shape[0]
    _row_copies(tc, lambda r: pltpu.make_async_copy(
        ys_hbm.at[pl.ds(pos_ref[0, 0, r], 1)], rows_ref.at[pl.ds(r, 1)], sem))
    out_ref[...] = h_ref[...] + rows_ref[...]


def _combine(pos3, h, ys):
    t, d = h.shape
    tc = pos3.shape[2]
    return pl.pallas_call(
        _combine_kernel,
        grid=(t // tc,),
        in_specs=[pl.BlockSpec((1, 1, tc), lambda i: (i, 0, 0), memory_space=pltpu.SMEM),
                  pl.BlockSpec((tc, d), lambda i: (i, 0)),
                  pl.BlockSpec(memory_space=pl.ANY)],
        out_specs=pl.BlockSpec((tc, d), lambda i: (i, 0)),
        out_shape=jax.ShapeDtypeStruct((t, d), F32),
        scratch_shapes=[pltpu.VMEM((tc, d), F32), pltpu.SemaphoreType.DMA(())],
        compiler_params=_params("arbitrary"),
        name="combine",
    )(pos3, h, ys)


def _kvq_kernel(h_ref, gkv_ref, gq_ref, w_ref, gain_ref, out_ref, xkv_ref, xq_ref, *, d):
    n = pl.program_id(1)
    bn = out_ref.shape[1]
    k_blocks, kv_blocks = d // bn, 2 * d // bn

    @pl.when(n == 0)
    def _():
        h = h_ref[...]
        xhat = h * _rms_scale(h)
        xkv_ref[...] = (xhat * gkv_ref[...]).astype(BF16)
        xq_ref[...] = (xhat * gq_ref[...]).astype(BF16)

    def project(x_ref, head_norm):
        acc = jnp.dot(x_ref[...], w_ref[...], preferred_element_type=F32)
        if head_norm:
            for hh in range(bn // HEAD_DIM):
                cols = slice(hh * HEAD_DIM, (hh + 1) * HEAD_DIM)
                blk = acc[:, cols]
                out_ref[:, cols] = (blk * _rms_scale(blk) * gain_ref[:, cols]).astype(BF16)
        else:
            out_ref[...] = acc.astype(BF16)

    @pl.when(n < k_blocks)
    def _():
        project(xkv_ref, True)

    @pl.when((n >= k_blocks) & (n < kv_blocks))
    def _():
        project(xkv_ref, False)

    @pl.when(n >= kv_blocks)
    def _():
        project(xq_ref, True)


def _kvq(h, gkv, gq, w_all, gain_all):
    t, d = h.shape
    n_out = w_all.shape[1]
    tm, bn = PROJ_TILE, PROJ_COLS
    return pl.pallas_call(
        functools.partial(_kvq_kernel, d=d),
        grid=(t // tm, n_out // bn),
        in_specs=[pl.BlockSpec((tm, d), lambda i, n: (i, 0)),
                  pl.BlockSpec((1, d), lambda i, n: (0, 0)),
                  pl.BlockSpec((1, d), lambda i, n: (0, 0)),
                  pl.BlockSpec((d, bn), lambda i, n: (0, n)),
                  pl.BlockSpec((1, bn), lambda i, n: (0, n))],
        out_specs=pl.BlockSpec((tm, bn), lambda i, n: (i, n)),
        out_shape=jax.ShapeDtypeStruct((t, n_out), BF16),
        scratch_shapes=[pltpu.VMEM((tm, d), BF16), pltpu.VMEM((tm, d), BF16)],
        compiler_params=_params("arbitrary", "arbitrary"),
        name="kvq",
    )(h, gkv, gq, w_all, gain_all)


def _attn_kernel(q_ref, k_ref, v_ref, o_ref):
    qi = pl.program_id(2)
    tq = q_ref.shape[0]
    tk = tq
    q = q_ref[...]
    rows = lax.broadcasted_iota(jnp.int32, (tq, tk), 0)
    cols = lax.broadcasted_iota(jnp.int32, (tq, tk), 1)
    later = (rows > cols).astype(BF16)
    causal = cols < rows

    def block(j, carry, acc, masked):
        start = pl.multiple_of(j * tk, tk)
        kb = k_ref[pl.ds(start, tk), :]
        vb = v_ref[pl.ds(start, tk), :]
        z = lax.dot_general(q, kb, (((1,), (1,)), ((), ())), preferred_element_type=F32)
        sp = jnp.maximum(z, 0.0) + jnp.log2(1.0 + jnp.exp2(-jnp.abs(z)))
        if masked:
            sp = jnp.where(causal, sp, 0.0)
        after = jnp.dot(sp.astype(BF16), later, preferred_element_type=F32)
        a = jnp.exp2(z - sp - after - carry)
        if masked:
            a = jnp.where(causal, a, 0.0)
        acc = acc + jnp.dot(a.astype(BF16), vb, preferred_element_type=F32)
        carry = carry + jnp.sum(sp, axis=1, keepdims=True)
        return carry, acc

    carry, acc = block(qi, jnp.zeros((tq, 1), F32), jnp.zeros((tq, HEAD_DIM), F32), True)

    def body(i, state):
        return block(qi - 1 - i, state[0], state[1], False)

    carry, acc = lax.fori_loop(0, qi, body, (carry, acc))
    o_ref[...] = acc.astype(BF16)


def _attention(kvq, batch, d):
    t = kvq.shape[0]
    seq = t // batch
    n_heads = d // HEAD_DIM
    tq = ATTN_TILE
    nq = seq // tq
    return pl.pallas_call(
        _attn_kernel,
        grid=(batch, n_heads, nq),
        in_specs=[pl.BlockSpec((tq, HEAD_DIM), lambda b, h, i: (b * nq + i, 2 * n_heads + h)),
                  pl.BlockSpec((seq, HEAD_DIM), lambda b, h, i: (b, h)),
                  pl.BlockSpec((seq, HEAD_DIM), lambda b, h, i: (b, n_heads + h))],
        out_specs=pl.BlockSpec((tq, HEAD_DIM), lambda b, h, i: (b * nq + i, h)),
        out_shape=jax.ShapeDtypeStruct((t, d), BF16),
        compiler_params=_params("arbitrary", "arbitrary", "arbitrary"),
        name="attention",
    )(kvq, kvq, kvq)


def _oproj_route_kernel(o_ref, wo_ref, h_ref, gffn_ref, wr_ref, br_ref,
                        hout_ref, xn_ref, rt_ref, cls_ref, *, n_grp, n_exp):
    hout_ref[...] = h_ref[...] + jnp.dot(o_ref[...], wo_ref[...], preferred_element_type=F32)
    _ffn_norm_route(hout_ref[...], gffn_ref, wr_ref, br_ref, xn_ref, rt_ref, cls_ref, n_grp, n_exp)


def _oproj_route(o, wo, h, gffn, wr, br, n_grp, n_exp):
    t, d = h.shape
    ts = TOKEN_TILE
    tok = lambda i: (i, 0)
    fixed2 = lambda i: (0, 0)
    return pl.pallas_call(
        functools.partial(_oproj_route_kernel, n_grp=n_grp, n_exp=n_exp),
        grid=(t // ts,),
        in_specs=[pl.BlockSpec((ts, d), tok),
                  pl.BlockSpec((d, d), fixed2),
                  pl.BlockSpec((ts, d), tok),
                  pl.BlockSpec((1, d), fixed2),
                  pl.BlockSpec((d, LANES), fixed2),
                  pl.BlockSpec((1, LANES), fixed2)],
        out_specs=(pl.BlockSpec((ts, d), tok),
                   pl.BlockSpec((ts, d), tok),
                   pl.BlockSpec((ts, LANES), tok),
                   pl.BlockSpec((1, SUBLANES, ts), lambda i: (i, 0, 0))),
        out_shape=_route_out(t, d, ts),
        compiler_params=_params("arbitrary"),
        name="oproj_route",
    )(o, wo, h, gffn, wr, br)


def _router_weights(w_grp, b_grp, w_rt, b_rt):
    d, n_grp = w_grp.shape
    n_exp = w_rt.shape[2]
    w = jnp.concatenate([w_grp, jnp.transpose(w_rt, (1, 0, 2)).reshape(d, n_grp * n_exp)], axis=1)
    b = jnp.concatenate([b_grp, b_rt.reshape(-1)])
    pad = LANES - w.shape[1]
    return (jnp.pad(w, ((0, 0), (0, pad))).astype(BF16), jnp.pad(b, (0, pad)).reshape(1, LANES))


def _pair_table(n_grp, n_exp):
    lo, hi = [], []
    for g in range(n_grp):
        for a in range(n_exp):
            for b in range(a + 1, n_exp):
                lo.append(g * n_exp + a)
                hi.append(g * n_exp + b)
    return jnp.array(lo, jnp.int32), jnp.array(hi, jnp.int32)


def _moe_layer(h, xn, rt, cls3, wg, wu, wd, n_grp, n_exp):
    t, d = h.shape
    tm = MOE_TILE
    cls_lo, cls_hi = _pair_table(n_grp, n_exp)
    n_cls = cls_lo.shape[0]
    n_tiles = t // tm + n_cls
    assert n_cls <= CLASS_ROWS and n_tiles < 256
    pos3, tiles = _positions(cls3, tm)
    tiles_c = tiles[:n_cls, 0].astype(jnp.int32)
    ends = jnp.cumsum(tiles_c)
    n_act = ends[-1]
    tail = jnp.where(tiles_c > 0, (ends - 1) * tm, -1).astype(jnp.int32)
    tile_ids = jnp.minimum(jnp.arange(n_tiles, dtype=jnp.int32), n_act - 1)
    tile_cls = jnp.sum(tile_ids[:, None] >= ends[None, :], axis=1)
    xs = _dispatch(tail, pos3, xn, rt, n_tiles * tm, tm)
    ys = _moe(cls_lo[tile_cls], cls_hi[tile_cls], n_act.reshape(1), xs, wg, wu, wd, tm)
    return _combine(pos3, h, ys)


def kernel(x, norm_mix, norm_ffn, pool_w, pool_scale, kv_norm, w_kv, k_norm, w_q, q_norm, w_o,
           moe_w_grp, moe_b_grp, moe_w_rt, moe_b_rt, moe_w_gate, moe_w_up, moe_w_down):
    batch, seq, d = x.shape
    t = batch * seq
    n_grp, n_exp = moe_w_rt.shape[1], moe_w_rt.shape[3]
    n_heads = d // HEAD_DIM
    assert len(POOL_WINDOWS) == pool_w.shape[1] and seq % TOKEN_TILE == 0 and seq % ATTN_TILE == 0
    assert t % PROJ_TILE == 0 and d % PROJ_COLS == 0 and n_grp + n_grp * n_exp <= LANES
    row = lambda v: v.reshape(1, -1).astype(F32)

    wr0, br0 = _router_weights(moe_w_grp[0], moe_b_grp[0], moe_w_rt[0], moe_b_rt[0])
    h1, xn, rt, cls3 = _pool_route(x.reshape(t, d), row(norm_mix[0]), pool_w[0].astype(BF16),
                                   row(pool_scale[0]), row(norm_ffn[0]), wr0, br0, batch, n_grp, n_exp)
    h2 = _moe_layer(h1, xn, rt, cls3, moe_w_gate[0].astype(BF16), moe_w_up[0].astype(BF16),
                    moe_w_down[0].astype(BF16), n_grp, n_exp)

    w_all = jnp.concatenate([w_kv, w_q[0]], axis=1).astype(BF16)
    q_gain = q_norm[0] * (HEAD_DIM ** -0.5 * LOG2E)
    gain_all = jnp.concatenate([jnp.tile(k_norm, n_heads), jnp.ones((d,), F32), jnp.tile(q_gain, n_heads)])
    kvq = _kvq(h2, row(kv_norm), row(norm_mix[1]), w_all, row(gain_all))
    o = _attention(kvq, batch, d)

    wr1, br1 = _router_weights(moe_w_grp[1], moe_b_grp[1], moe_w_rt[1], moe_b_rt[1])
    h3, xn, rt, cls3 = _oproj_route(o, w_o[0].astype(BF16), h2, row(norm_ffn[1]), wr1, br1, n_grp, n_exp)
    h4 = _moe_layer(h3, xn, rt, cls3, moe_w_gate[1].astype(BF16), moe_w_up[1].astype(BF16),
                    moe_w_down[1].astype(BF16), n_grp, n_exp)
    return h4.reshape(batch, seq, d)
```

```python
import functools

import jax
import jax.numpy as jnp
from jax import lax
from jax.experimental import pallas as pl
from jax.experimental.pallas import tpu as pltpu

F32 = jnp.float32
BF16 = jnp.bfloat16
U32 = jnp.uint32

EPS = 1e-6
POOL_WINDOWS = (2, 4, 8, 16)
POOL_HALO = 16
HEAD_DIM = 128
LANES = 128
SUBLANES = 8
MXU_COLS = 256
LOG2E = 1.4426950408889634

TOKEN_TILE = 256
MOE_TILE = 256
PROJ_TILE = 512
PROJ_COLS = 1024
ATTN_TILE = 256
ATTN_HEADS = 4
CLASS_ROWS = 32
DMA_UNROLL = 8
VMEM_LIMIT = 56 * 1024 * 1024


def _params(*sem):
    return pltpu.CompilerParams(dimension_semantics=sem, vmem_limit_bytes=VMEM_LIMIT)


def _rms_scale(x):
    return lax.rsqrt(jnp.mean(x * x, axis=-1, keepdims=True) + EPS)


def _pack_pairs(hi_bf16, lo_bf16):
    hi = lax.bitcast_convert_type(hi_bf16.astype(F32), U32)
    lo = lax.bitcast_convert_type(lo_bf16.astype(F32), U32)
    return hi | (lo >> 16)


def _unpack_pairs(words):
    hi = lax.bitcast_convert_type(words & jnp.uint32(0xFFFF0000), F32)
    lo = lax.bitcast_convert_type(words << 16, F32)
    return hi, lo


def _first_argmax(vals):
    best, idx = vals[0], jnp.zeros(vals[0].shape, jnp.int32)
    for i in range(1, len(vals)):
        upd = vals[i] > best
        best = jnp.where(upd, vals[i], best)
        idx = jnp.where(upd, i, idx)
    return best, idx


def _ffn_norm_route(h, gffn_ref, wr_ref, br_ref, xr_ref, cls_ref, n_grp, n_exp):
    ts, d = h.shape
    half = d // 2
    xnb = (h * _rms_scale(h) * gffn_ref[...]).astype(BF16)
    xr_ref[:, :half] = _pack_pairs(xnb[:, :half], xnb[:, half:])
    logits = jnp.dot(xnb, wr_ref[...], preferred_element_type=F32) + br_ref[...]
    lt = logits.T
    row = lambda i: lt[i:i + 1, :]
    g_best, g_idx = _first_argmax([row(g) for g in range(n_grp)])
    denom = sum(jnp.exp(row(g) - g_best) for g in range(n_grp))
    g_w = 1.0 / denom
    sel = []
    for e in range(n_exp):
        v = row(n_grp + e)
        for g in range(1, n_grp):
            v = jnp.where(g_idx == g, row(n_grp + g * n_exp + e), v)
        sel.append(v)
    v1, i1 = _first_argmax(sel)
    v2 = jnp.full_like(v1, -jnp.inf)
    i2 = jnp.zeros_like(i1)
    for e in range(n_exp):
        upd = (i1 != e) & (sel[e] > v2)
        v2 = jnp.where(upd, sel[e], v2)
        i2 = jnp.where(upd, e, i2)
    t = jnp.exp(v2 - v1)
    w1 = g_w / (1.0 + t)
    w2 = w1 * t
    first_low = i1 < i2
    lo = jnp.where(first_low, i1, i2)
    hi = jnp.where(first_low, i2, i1)
    w_lo = jnp.where(first_low, w1, w2)
    w_hi = jnp.where(first_low, w2, w1)
    n_pairs = n_exp * (n_exp - 1) // 2
    pair = ((lo * (2 * n_exp - 1 - lo)) >> 1) + (hi - lo - 1)
    cls = (g_idx * n_pairs + pair).astype(F32)
    r = lax.broadcasted_iota(jnp.int32, (LANES, ts), 0)
    packed = jnp.where(r == 0, cls, jnp.where(r == 1, w_lo, jnp.where(r == 2, w_hi, 0.0)))
    cls_ref[0] = packed[:SUBLANES, :]
    xr_ref[:, half:] = lax.bitcast_convert_type(packed.T, U32)


def _route_out(t, d, ts):
    return (jax.ShapeDtypeStruct((t, d), F32),
            jax.ShapeDtypeStruct((t, d // 2 + LANES), U32),
            jax.ShapeDtypeStruct((t // ts, SUBLANES, ts), F32))


def _pool_route_kernel(x_ref, gmix_ref, pw_ref, pscale_ref, gffn_ref, wr_ref, br_ref,
                       h_ref, xr_ref, cls_ref, halo_ref, *, n_grp, n_exp):
    s = pl.program_id(1)
    ts, d = x_ref.shape
    c = d // len(POOL_WINDOWS)
    x = x_ref[...]
    xn = x * _rms_scale(x) * gmix_ref[...]

    @pl.when(s == 0)
    def _():
        halo_ref[...] = jnp.zeros_like(halo_ref)

    ext = jnp.concatenate([halo_ref[...], xn], axis=0)
    halo_ref[...] = xn[ts - POOL_HALO:, :]
    pos = s * ts + lax.broadcasted_iota(jnp.int32, (ts, LANES), 0)
    for gi, w in enumerate(POOL_WINDOWS):
        acc = ext[:, gi * c:(gi + 1) * c]
        k = 1
        while k < w:
            acc = acc + pltpu.roll(acc, k, 0)
            k *= 2
        inv_cnt = 1.0 / jnp.minimum(pos + 1, w).astype(F32)
        inv_cnt = jnp.concatenate([inv_cnt] * (c // LANES), axis=1)
        pooled = acc[POOL_HALO:, :] * inv_cnt - xn[:, gi * c:(gi + 1) * c]
        y = jnp.dot(pooled.astype(BF16), pw_ref[gi], preferred_element_type=F32)
        h_ref[:, gi * c:(gi + 1) * c] = (x[:, gi * c:(gi + 1) * c]
                                         + y * pscale_ref[:, gi * c:(gi + 1) * c])
    _ffn_norm_route(h_ref[...], gffn_ref, wr_ref, br_ref, xr_ref, cls_ref, n_grp, n_exp)


def _pool_route(x2, gmix, pw, pscale, gffn, wr, br, batch, n_grp, n_exp):
    t, d = x2.shape
    ts = TOKEN_TILE
    seq_tiles = t // batch // ts
    tok = lambda b, s: (b * seq_tiles + s, 0)
    fixed2 = lambda b, s: (0, 0)
    return pl.pallas_call(
        functools.partial(_pool_route_kernel, n_grp=n_grp, n_exp=n_exp),
        grid=(batch, seq_tiles),
        in_specs=[pl.BlockSpec((ts, d), tok),
                  pl.BlockSpec((1, d), fixed2),
                  pl.BlockSpec(pw.shape, lambda b, s: (0, 0, 0)),
                  pl.BlockSpec((1, d), fixed2),
                  pl.BlockSpec((1, d), fixed2),
                  pl.BlockSpec((d, LANES), fixed2),
                  pl.BlockSpec((1, LANES), fixed2)],
        out_specs=(pl.BlockSpec((ts, d), tok),
                   pl.BlockSpec((ts, d // 2 + LANES), tok),
                   pl.BlockSpec((1, SUBLANES, ts), lambda b, s: (b * seq_tiles + s, 0, 0))),
        out_shape=_route_out(t, d, ts),
        scratch_shapes=[pltpu.VMEM((POOL_HALO, d), F32)],
        compiler_params=_params("arbitrary", "arbitrary"),
        name="pool_route",
    )(x2, gmix, pw, pscale, gffn, wr, br)


def _positions_kernel(cls_ref, pos_ref, tiles_ref, *, tm):
    n_t, _, ts = cls_ref.shape
    ci = lax.broadcasted_iota(jnp.int32, (CLASS_ROWS, ts), 0).astype(F32)

    def onehot(i):
        return (cls_ref[i, 0:1, :] == ci).astype(F32)

    acc = lax.fori_loop(0, n_t, lambda i, a: a + onehot(i), jnp.zeros((CLASS_ROWS, ts), F32))
    counts = jnp.sum(acc, axis=1, keepdims=True)
    tiles = jnp.floor((counts + (tm - 1)) * (1.0 / tm))
    tiles_b = jnp.broadcast_to(tiles, (CLASS_ROWS, LANES))
    tiles_ref[...] = tiles_b
    rr = lax.broadcasted_iota(jnp.int32, (CLASS_ROWS, CLASS_ROWS), 0)
    cc = lax.broadcasted_iota(jnp.int32, (CLASS_ROWS, CLASS_ROWS), 1)
    before = (cc < rr).astype(BF16)
    start = jnp.dot(before, tiles_b.astype(BF16), preferred_element_type=F32)[:, 0:1] * tm
    jj = lax.broadcasted_iota(jnp.int32, (ts, ts), 0)
    ss = lax.broadcasted_iota(jnp.int32, (ts, ts), 1)
    earlier = (jj < ss).astype(BF16)

    def body(i, run):
        oh = onehot(i)
        rank = jnp.dot(oh.astype(BF16), earlier, preferred_element_type=F32)
        pos_ref[i] = jnp.sum(oh * (run + rank), axis=0, keepdims=True).astype(jnp.int32)
        return run + jnp.sum(oh, axis=1, keepdims=True)

    lax.fori_loop(0, n_t, body, start)


def _positions(cls3, tm):
    n_t, _, ts = cls3.shape
    return pl.pallas_call(
        functools.partial(_positions_kernel, tm=tm),
        out_shape=(jax.ShapeDtypeStruct((n_t, 1, ts), jnp.int32),
                   jax.ShapeDtypeStruct((CLASS_ROWS, LANES), F32)),
        compiler_params=pltpu.CompilerParams(vmem_limit_bytes=VMEM_LIMIT),
        name="positions",
    )(cls3)


def _start_rows(n, make):
    def start(r, _):
        make(r).start()
        return 0

    lax.fori_loop(0, n, start, 0, unroll=DMA_UNROLL)


def _dispatch_kernel(tail_ref, pos_ref, xr_ref, xs_hbm, rows_ref, zero_ref, sems, zsem, *, tm, n_tiles):
    i = pl.program_id(0)
    n_steps = pl.num_programs(0)
    td = xr_ref.shape[0]
    n_cls = tail_ref.shape[0] - 1
    slot = i % 2

    @pl.when(i == 0)
    def _():
        zero_ref[...] = jnp.zeros_like(zero_ref)
        n_act = tail_ref[n_cls]

        def zero_tile(start):
            return pltpu.make_async_copy(zero_ref, xs_hbm.at[pl.ds(pl.multiple_of(start, tm), tm)], zsem)

        todo = [(tail_ref[c] >= 0, jnp.maximum(tail_ref[c], 0)) for c in range(n_cls)]
        todo += [(n_act + c < n_tiles, jnp.minimum(n_act + c, n_tiles - 1) * tm) for c in range(n_cls)]
        for needed, start in todo:
            @pl.when(needed)
            def _():
                zero_tile(start).start()
        for needed, start in todo:
            @pl.when(needed)
            def _():
                zero_tile(start).wait()

    def wait_rows(s):
        pltpu.make_async_copy(rows_ref.at[s], xs_hbm.at[pl.ds(0, td)], sems.at[s]).wait()

    rows_ref[slot] = xr_ref[...]
    _start_rows(td, lambda r: pltpu.make_async_copy(
        rows_ref.at[slot, pl.ds(r, 1)], xs_hbm.at[pl.ds(pos_ref[0, 0, r], 1)], sems.at[slot]))

    @pl.when(i > 0)
    def _():
        wait_rows(1 - slot)

    @pl.when(i == n_steps - 1)
    def _():
        wait_rows(slot)


def _dispatch(tail, pos3, xr, n_tiles, tm):
    t, dw = xr.shape
    td = pos3.shape[2]
    return pl.pallas_call(
        functools.partial(_dispatch_kernel, tm=tm, n_tiles=n_tiles),
        grid_spec=pltpu.PrefetchScalarGridSpec(
            num_scalar_prefetch=1,
            grid=(t // td,),
            in_specs=[pl.BlockSpec((1, 1, td), lambda i, tail: (i, 0, 0), memory_space=pltpu.SMEM),
                      pl.BlockSpec((td, dw), lambda i, tail: (i, 0))],
            out_specs=pl.BlockSpec(memory_space=pl.ANY),
            scratch_shapes=[pltpu.VMEM((2, td, dw), U32),
                            pltpu.VMEM((tm, dw), U32),
                            pltpu.SemaphoreType.DMA((2,)),
                            pltpu.SemaphoreType.DMA(())]),
        out_shape=jax.ShapeDtypeStruct((n_tiles * tm, dw), U32),
        compiler_params=_params("arbitrary"),
        name="dispatch",
    )(tail, pos3, xr)


def _moe_kernel(elo_ref, ehi_ref, nact_ref, xs_ref, wg_lo, wg_hi, wu_lo, wu_hi, wd_lo, wd_hi, ys_ref):
    half = ys_ref.shape[1]
    active = pl.program_id(0) < nact_ref[0]

    @pl.when(active)
    def _():
        x_hi, x_lo = _unpack_pairs(xs_ref[:, :half])
        x = jnp.concatenate([x_hi.astype(BF16), x_lo.astype(BF16)], axis=1)

        def expert(wg, wu, lane):
            g = jnp.dot(x, wg[0], preferred_element_type=F32)
            u = jnp.dot(x, wu[0], preferred_element_type=F32)
            gate = lax.bitcast_convert_type(xs_ref[:, half + lane:half + lane + 1], F32)
            return (g * (1.0 / (1.0 + jnp.exp(-g))) * u * gate).astype(BF16)

        y = (jnp.dot(expert(wg_lo, wu_lo, 1), wd_lo[0], preferred_element_type=F32)
             + jnp.dot(expert(wg_hi, wu_hi, 2), wd_hi[0], preferred_element_type=F32))
        ys_ref[...] = _pack_pairs(y[:, :half].astype(BF16), y[:, half:].astype(BF16))

    @pl.when(jnp.logical_not(active))
    def _():
        ys_ref[...] = jnp.zeros_like(ys_ref)


def _moe(e_lo, e_hi, n_act, xs, wg, wu, wd, tm):
    n_rows, dw = xs.shape
    half = dw - LANES
    d = 2 * half
    f = wg.shape[2]
    tile = lambda i, lo, hi, na: (jnp.minimum(i, na[0] - 1), 0)
    w_lo = lambda i, lo, hi, na: (lo[i], 0, 0)
    w_hi = lambda i, lo, hi, na: (hi[i], 0, 0)
    return pl.pallas_call(
        _moe_kernel,
        grid_spec=pltpu.PrefetchScalarGridSpec(
            num_scalar_prefetch=3,
            grid=(n_rows // tm,),
            in_specs=[pl.BlockSpec((tm, dw), tile),
                      pl.BlockSpec((1, d, f), w_lo), pl.BlockSpec((1, d, f), w_hi),
                      pl.BlockSpec((1, d, f), w_lo), pl.BlockSpec((1, d, f), w_hi),
                      pl.BlockSpec((1, f, d), w_lo), pl.BlockSpec((1, f, d), w_hi)],
            out_specs=pl.BlockSpec((tm, half), lambda i, lo, hi, na: (i, 0))),
        out_shape=jax.ShapeDtypeStruct((n_rows, half), U32),
        compiler_params=_params("arbitrary"),
        name="moe",
    )(e_lo, e_hi, n_act, xs, wg, wg, wu, wu, wd, wd)


def _combine_kernel(pos_ref, posn_ref, h_ref, ys_hbm, out_ref, rows_ref, sems):
    i = pl.program_id(0)
    n_steps = pl.num_programs(0)
    tc = h_ref.shape[0]
    slot = i % 2

    def gather(p_ref, s):
        _start_rows(tc, lambda r: pltpu.make_async_copy(
            ys_hbm.at[pl.ds(p_ref[0, 0, r], 1)], rows_ref.at[s, pl.ds(r, 1)], sems.at[s]))

    @pl.when(i == 0)
    def _():
        gather(pos_ref, 0)

    @pl.when(i + 1 < n_steps)
    def _():
        gather(posn_ref, 1 - slot)

    pltpu.make_async_copy(ys_hbm.at[pl.ds(0, tc)], rows_ref.at[slot], sems.at[slot]).wait()
    y_hi, y_lo = _unpack_pairs(rows_ref[slot])
    out_ref[...] = h_ref[...] + jnp.concatenate([y_hi, y_lo], axis=1)


def _combine(pos3, h, ys):
    t, d = h.shape
    n_steps, _, tc = pos3.shape
    return pl.pallas_call(
        _combine_kernel,
        grid=(n_steps,),
        in_specs=[pl.BlockSpec((1, 1, tc), lambda i: (i, 0, 0), memory_space=pltpu.SMEM),
                  pl.BlockSpec((1, 1, tc), lambda i: (jnp.minimum(i + 1, n_steps - 1), 0, 0),
                               memory_space=pltpu.SMEM),
                  pl.BlockSpec((tc, d), lambda i: (i, 0)),
                  pl.BlockSpec(memory_space=pl.ANY)],
        out_specs=pl.BlockSpec((tc, d), lambda i: (i, 0)),
        out_shape=jax.ShapeDtypeStruct((t, d), F32),
        scratch_shapes=[pltpu.VMEM((2, tc, d // 2), U32), pltpu.SemaphoreType.DMA((2,))],
        compiler_params=_params("arbitrary"),
        name="combine",
    )(pos3, pos3, h, ys)


def _kvq_kernel(h_ref, gkv_ref, gq_ref, w_ref, gain_ref, out_ref, xkv_ref, xq_ref, *, d):
    n = pl.program_id(1)
    bn = out_ref.shape[1]
    k_blocks, kv_blocks = d // bn, 2 * d // bn

    @pl.when(n == 0)
    def _():
        h = h_ref[...]
        xhat = h * _rms_scale(h)
        xkv_ref[...] = (xhat * gkv_ref[...]).astype(BF16)
        xq_ref[...] = (xhat * gq_ref[...]).astype(BF16)

    def project(x_ref, head_norm):
        x = x_ref[...]
        for c0 in range(0, bn, MXU_COLS):
            acc = jnp.dot(x, w_ref[:, c0:c0 + MXU_COLS], preferred_element_type=F32)
            if head_norm:
                for hh in range(MXU_COLS // HEAD_DIM):
                    cols = slice(c0 + hh * HEAD_DIM, c0 + (hh + 1) * HEAD_DIM)
                    blk = acc[:, hh * HEAD_DIM:(hh + 1) * HEAD_DIM]
                    out_ref[:, cols] = (blk * _rms_scale(blk) * gain_ref[:, cols]).astype(BF16)
            else:
                out_ref[:, c0:c0 + MXU_COLS] = acc.astype(BF16)

    @pl.when(n < k_blocks)
    def _():
        project(xkv_ref, True)

    @pl.when((n >= k_blocks) & (n < kv_blocks))
    def _():
        project(xkv_ref, False)

    @pl.when(n >= kv_blocks)
    def _():
        project(xq_ref, True)


def _kvq(h, gkv, gq, w_all, gain_all):
    t, d = h.shape
    n_out = w_all.shape[1]
    tm, bn = PROJ_TILE, PROJ_COLS
    return pl.pallas_call(
        functools.partial(_kvq_kernel, d=d),
        grid=(t // tm, n_out // bn),
        in_specs=[pl.BlockSpec((tm, d), lambda i, n: (i, 0)),
                  pl.BlockSpec((1, d), lambda i, n: (0, 0)),
                  pl.BlockSpec((1, d), lambda i, n: (0, 0)),
                  pl.BlockSpec((d, bn), lambda i, n: (0, n)),
                  pl.BlockSpec((1, bn), lambda i, n: (0, n))],
        out_specs=pl.BlockSpec((tm, bn), lambda i, n: (i, n)),
        out_shape=jax.ShapeDtypeStruct((t, n_out), BF16),
        scratch_shapes=[pltpu.VMEM((tm, d), BF16), pltpu.VMEM((tm, d), BF16)],
        compiler_params=_params("arbitrary", "arbitrary"),
        name="kvq",
    )(h, gkv, gq, w_all, gain_all)


def _attn_kernel(q_ref, k_ref, v_ref, o_ref):
    qi = pl.program_id(2)
    tq = q_ref.shape[0]
    tk = tq
    n_heads = q_ref.shape[1] // HEAD_DIM
    rows = lax.broadcasted_iota(jnp.int32, (tq, tk), 0)
    cols = lax.broadcasted_iota(jnp.int32, (tq, tk), 1)
    later = (rows > cols).astype(BF16)
    causal = cols < rows
    sign = jnp.uint32(0x80000000)

    def block(g, j, carry, acc, masked):
        hd = slice(g * HEAD_DIM, (g + 1) * HEAD_DIM)
        start = pl.multiple_of(j * tk, tk)
        z = lax.dot_general(q_ref[:, hd], k_ref[pl.ds(start, tk), hd], (((1,), (1,)), ((), ())),
                            preferred_element_type=F32)
        neg_abs = lax.bitcast_convert_type(lax.bitcast_convert_type(z, U32) | sign, F32)
        sp = jnp.maximum(z, 0.0) + jnp.log2(1.0 + jnp.exp2(neg_abs))
        if masked:
            sp = jnp.where(causal, sp, 0.0)
        after = jnp.dot(sp.astype(BF16), later, preferred_element_type=F32)
        a = jnp.exp2(z - sp - after - carry)
        if masked:
            a = jnp.where(causal, a, 0.0)
        acc = acc + jnp.dot(a.astype(BF16), v_ref[pl.ds(start, tk), hd], preferred_element_type=F32)
        carry = carry + jnp.sum(sp, axis=1, keepdims=True)
        return carry, acc

    state = tuple(block(g, qi, jnp.zeros((tq, 1), F32), jnp.zeros((tq, HEAD_DIM), F32), True)
                  for g in range(n_heads))

    def body(i, state):
        return tuple(block(g, qi - 1 - i, state[g][0], state[g][1], False) for g in range(n_heads))

    state = lax.fori_loop(0, qi, body, state)
    for g in range(n_heads):
        o_ref[:, g * HEAD_DIM:(g + 1) * HEAD_DIM] = state[g][1].astype(BF16)


def _attention(kvq, batch, d):
    t = kvq.shape[0]
    seq = t // batch
    gw = ATTN_HEADS * HEAD_DIM
    n_hg = d // gw
    tq = ATTN_TILE
    nq = seq // tq
    return pl.pallas_call(
        _attn_kernel,
        grid=(batch, n_hg, nq),
        in_specs=[pl.BlockSpec((tq, gw), lambda b, h, i: (b * nq + i, 2 * n_hg + h)),
                  pl.BlockSpec((seq, gw), lambda b, h, i: (b, h)),
                  pl.BlockSpec((seq, gw), lambda b, h, i: (b, n_hg + h))],
        out_specs=pl.BlockSpec((tq, gw), lambda b, h, i: (b * nq + i, h)),
        out_shape=jax.ShapeDtypeStruct((t, d), BF16),
        compiler_params=_params("arbitrary", "arbitrary", "arbitrary"),
        name="attention",
    )(kvq, kvq, kvq)


def _oproj_route_kernel(o_ref, wo_ref, h_ref, gffn_ref, wr_ref, br_ref,
                        hout_ref, xr_ref, cls_ref, *, n_grp, n_exp):
    hout_ref[...] = h_ref[...] + jnp.dot(o_ref[...], wo_ref[...], preferred_element_type=F32)
    _ffn_norm_route(hout_ref[...], gffn_ref, wr_ref, br_ref, xr_ref, cls_ref, n_grp, n_exp)


def _oproj_route(o, wo, h, gffn, wr, br, n_grp, n_exp):
    t, d = h.shape
    ts = TOKEN_TILE
    tok = lambda i: (i, 0)
    fixed2 = lambda i: (0, 0)
    return pl.pallas_call(
        functools.partial(_oproj_route_kernel, n_grp=n_grp, n_exp=n_exp),
        grid=(t // ts,),
        in_specs=[pl.BlockSpec((ts, d), tok),
                  pl.BlockSpec((d, d), fixed2),
                  pl.BlockSpec((ts, d), tok),
                  pl.BlockSpec((1, d), fixed2),
                  pl.BlockSpec((d, LANES), fixed2),
                  pl.BlockSpec((1, LANES), fixed2)],
        out_specs=(pl.BlockSpec((ts, d), tok),
                   pl.BlockSpec((ts, d // 2 + LANES), tok),
                   pl.BlockSpec((1, SUBLANES, ts), lambda i: (i, 0, 0))),
        out_shape=_route_out(t, d, ts),
        compiler_params=_params("arbitrary"),
        name="oproj_route",
    )(o, wo, h, gffn, wr, br)


def _router_weights(w_grp, b_grp, w_rt, b_rt):
    d, n_grp = w_grp.shape
    n_exp = w_rt.shape[2]
    w = jnp.concatenate([w_grp, jnp.transpose(w_rt, (1, 0, 2)).reshape(d, n_grp * n_exp)], axis=1)
    b = jnp.concatenate([b_grp, b_rt.reshape(-1)])
    pad = LANES - w.shape[1]
    return (jnp.pad(w, ((0, 0), (0, pad))).astype(BF16), jnp.pad(b, (0, pad)).reshape(1, LANES))


def _pair_table(n_grp, n_exp):
    lo, hi = [], []
    for g in range(n_grp):
        for a in range(n_exp):
            for b in range(a + 1, n_exp):
                lo.append(g * n_exp + a)
                hi.append(g * n_exp + b)
    return jnp.array(lo, jnp.int32), jnp.array(hi, jnp.int32)


def _moe_layer(h, xr, cls3, wg, wu, wd, n_grp, n_exp):
    t, d = h.shape
    tm = MOE_TILE
    cls_lo, cls_hi = _pair_table(n_grp, n_exp)
    n_cls = cls_lo.shape[0]
    n_tiles = t // tm + n_cls
    assert n_cls <= CLASS_ROWS and n_tiles < 256
    pos3, tiles = _positions(cls3, tm)
    tiles_c = tiles[:n_cls, 0].astype(jnp.int32)
    ends = jnp.cumsum(tiles_c)
    n_act = ends[-1]
    tail = jnp.where(tiles_c > 0, (ends - 1) * tm, -1).astype(jnp.int32)
    tile_ids = jnp.minimum(jnp.arange(n_tiles, dtype=jnp.int32), n_act - 1)
    tile_cls = jnp.sum(tile_ids[:, None] >= ends[None, :], axis=1)
    xs = _dispatch(jnp.concatenate([tail, n_act.reshape(1)]), pos3, xr, n_tiles, tm)
    ys = _moe(cls_lo[tile_cls], cls_hi[tile_cls], n_act.reshape(1), xs, wg, wu, wd, tm)
    return _combine(pos3, h, ys)


def kernel(x, norm_mix, norm_ffn, pool_w, pool_scale, kv_norm, w_kv, k_norm, w_q, q_norm, w_o,
           moe_w_grp, moe_b_grp, moe_w_rt, moe_b_rt, moe_w_gate, moe_w_up, moe_w_down):
    batch, seq, d = x.shape
    t = batch * seq
    n_grp, n_exp = moe_w_rt.shape[1], moe_w_rt.shape[3]
    n_heads = d // HEAD_DIM
    assert len(POOL_WINDOWS) == pool_w.shape[1] and seq % TOKEN_TILE == 0 and seq % ATTN_TILE == 0
    assert t % PROJ_TILE == 0 and d % PROJ_COLS == 0 and n_grp + n_grp * n_exp <= LANES
    assert n_heads % ATTN_HEADS == 0
    row = lambda v: v.reshape(1, -1).astype(F32)

    wr0, br0 = _router_weights(moe_w_grp[0], moe_b_grp[0], moe_w_rt[0], moe_b_rt[0])
    h1, xr, cls3 = _pool_route(x.reshape(t, d), row(norm_mix[0]), pool_w[0].astype(BF16),
                               row(pool_scale[0]), row(norm_ffn[0]), wr0, br0, batch, n_grp, n_exp)
    h2 = _moe_layer(h1, xr, cls3, moe_w_gate[0].astype(BF16), moe_w_up[0].astype(BF16),
                    moe_w_down[0].astype(BF16), n_grp, n_exp)

    w_all = jnp.concatenate([w_kv, w_q[0]], axis=1).astype(BF16)
    q_gain = q_norm[0] * (HEAD_DIM ** -0.5 * LOG2E)
    gain_all = jnp.concatenate([jnp.tile(k_norm, n_heads), jnp.ones((d,), F32), jnp.tile(q_gain, n_heads)])
    kvq = _kvq(h2, row(kv_norm), row(norm_mix[1]), w_all, row(gain_all))
    o = _attention(kvq, batch, d)

    wr1, br1 = _router_weights(moe_w_grp[1], moe_b_grp[1], moe_w_rt[1], moe_b_rt[1])
    h3, xr, cls3 = _oproj_route(o, w_o[0].astype(BF16), h2, row(norm_ffn[1]), wr1, br1, n_grp, n_exp)
    h4 = _moe_layer(h3, xr, cls3, moe_w_gate[1].astype(BF16), moe_w_up[1].astype(BF16),
                    moe_w_down[1].astype(BF16), n_grp, n_exp)
    return h4.reshape(batch, seq, d)
```

```python
import functools

import jax
import jax.numpy as jnp
from jax import lax
from jax.experimental import pallas as pl
from jax.experimental.pallas import tpu as pltpu

F32 = jnp.float32
BF16 = jnp.bfloat16
WORD = jnp.uint32

EPS = 1e-6
POOL_WINDOWS = (2, 4, 8, 16)
POOL_HALO = 16
HEAD_DIM = 128
LANES = 128
SUBLANES = 8
MXU_COLS = 256
LOG2E = 1.4426950408889634

TOKEN_TILE = 256
ROUTE_STEP = 512
MOE_TILE = 256
PROJ_TILE = 1024
PROJ_COLS = 1024
ATTN_TILE = 256
ATTN_KEYS = 256
ATTN_HEADS = 8
CLASS_ROWS = 32
DMA_UNROLL = 8
VMEM_LIMIT = 56 * 1024 * 1024


def _params(*sem):
    return pltpu.CompilerParams(dimension_semantics=sem, vmem_limit_bytes=VMEM_LIMIT)


def _rms_scale(x):
    return lax.rsqrt(jnp.mean(x * x, axis=-1, keepdims=True) + EPS)


def _pack_pairs(first, second):
    return pltpu.pack_elementwise([first, second], packed_dtype=BF16)


def _unpack_pairs(words):
    return tuple(pltpu.unpack_elementwise(words, index=i, packed_dtype=BF16, unpacked_dtype=F32)
                 for i in range(2))


def _first_argmax(vals):
    best, idx = vals[0], jnp.zeros(vals[0].shape, jnp.int32)
    for i in range(1, len(vals)):
        upd = vals[i] > best
        best = jnp.where(upd, vals[i], best)
        idx = jnp.where(upd, i, idx)
    return best, idx


def _ffn_norm_route(h, gffn_ref, wr_ref, br_ref, xr_ref, cls_ref, sub, n_grp, n_exp):
    ts, d = h.shape
    half = d // 2
    rows = pl.ds(sub * ts, ts)
    xn = h * _rms_scale(h) * gffn_ref[...]
    xr_ref[rows, :half] = _pack_pairs(xn[:, :half], xn[:, half:])
    logits = jnp.dot(xn.astype(BF16), wr_ref[...], preferred_element_type=F32) + br_ref[...]
    lt = logits.T
    row = lambda i: lt[i:i + 1, :]
    g_best, g_idx = _first_argmax([row(g) for g in range(n_grp)])
    denom = sum(jnp.exp(row(g) - g_best) for g in range(n_grp))
    g_w = 1.0 / denom
    sel = []
    for e in range(n_exp):
        v = row(n_grp + e)
        for g in range(1, n_grp):
            v = jnp.where(g_idx == g, row(n_grp + g * n_exp + e), v)
        sel.append(v)
    v1, i1 = _first_argmax(sel)
    v2 = jnp.full_like(v1, -jnp.inf)
    i2 = jnp.zeros_like(i1)
    for e in range(n_exp):
        upd = (i1 != e) & (sel[e] > v2)
        v2 = jnp.where(upd, sel[e], v2)
        i2 = jnp.where(upd, e, i2)
    t = jnp.exp(v2 - v1)
    w1 = g_w / (1.0 + t)
    w2 = w1 * t
    first_low = i1 < i2
    lo = jnp.where(first_low, i1, i2)
    hi = jnp.where(first_low, i2, i1)
    w_lo = jnp.where(first_low, w1, w2)
    w_hi = jnp.where(first_low, w2, w1)
    n_pairs = n_exp * (n_exp - 1) // 2
    pair = ((lo * (2 * n_exp - 1 - lo)) >> 1) + (hi - lo - 1)
    cls = (g_idx * n_pairs + pair).astype(F32)
    r = lax.broadcasted_iota(jnp.int32, (LANES, ts), 0)
    packed = jnp.where(r == 0, cls, jnp.where(r == 1, w_lo, jnp.where(r == 2, w_hi, 0.0)))
    cls_ref[sub] = packed[:SUBLANES, :]
    xr_ref[rows, half:] = lax.bitcast_convert_type(packed.T, xr_ref.dtype)


def _route_out(t, d, ts):
    return (jax.ShapeDtypeStruct((t, d), F32),
            jax.ShapeDtypeStruct((t, d // 2 + LANES), WORD),
            jax.ShapeDtypeStruct((t // ts, SUBLANES, ts), F32))


def _pool_route_kernel(x_ref, gmix_ref, pw_ref, pscale_ref, gffn_ref, wr_ref, br_ref,
                       h_ref, xr_ref, cls_ref, halo_ref, *, n_grp, n_exp):
    s = pl.program_id(1)
    d = x_ref.shape[1]
    ts = TOKEN_TILE
    c = d // len(POOL_WINDOWS)

    @pl.when(s == 0)
    def _():
        halo_ref[...] = jnp.zeros_like(halo_ref)

    for sub in range(x_ref.shape[0] // ts):
        rows = pl.ds(sub * ts, ts)
        x = x_ref[rows, :]
        xn = x * _rms_scale(x) * gmix_ref[...]
        ext = jnp.concatenate([halo_ref[...], xn], axis=0)
        halo_ref[...] = xn[ts - POOL_HALO:, :]
        pos = (s * x_ref.shape[0] + sub * ts) + lax.broadcasted_iota(jnp.int32, (ts, LANES), 0)
        for gi, w in enumerate(POOL_WINDOWS):
            acc = ext[:, gi * c:(gi + 1) * c]
            k = 1
            while k < w:
                acc = acc + pltpu.roll(acc, k, 0)
                k *= 2
            inv_cnt = 1.0 / jnp.minimum(pos + 1, w).astype(F32)
            inv_cnt = jnp.concatenate([inv_cnt] * (c // LANES), axis=1)
            pooled = acc[POOL_HALO:, :] * inv_cnt - xn[:, gi * c:(gi + 1) * c]
            y = jnp.dot(pooled.astype(BF16), pw_ref[gi], preferred_element_type=F32)
            h_ref[rows, gi * c:(gi + 1) * c] = (x[:, gi * c:(gi + 1) * c]
                                                + y * pscale_ref[:, gi * c:(gi + 1) * c])
        _ffn_norm_route(h_ref[rows, :], gffn_ref, wr_ref, br_ref, xr_ref, cls_ref, sub, n_grp, n_exp)


def _pool_route(x2, gmix, pw, pscale, gffn, wr, br, batch, n_grp, n_exp):
    t, d = x2.shape
    ts, step = TOKEN_TILE, ROUTE_STEP
    seq_steps = t // batch // step
    tok = lambda b, s: (b * seq_steps + s, 0)
    fixed2 = lambda b, s: (0, 0)
    return pl.pallas_call(
        functools.partial(_pool_route_kernel, n_grp=n_grp, n_exp=n_exp),
        grid=(batch, seq_steps),
        in_specs=[pl.BlockSpec((step, d), tok),
                  pl.BlockSpec((1, d), fixed2),
                  pl.BlockSpec(pw.shape, lambda b, s: (0, 0, 0)),
                  pl.BlockSpec((1, d), fixed2),
                  pl.BlockSpec((1, d), fixed2),
                  pl.BlockSpec((d, LANES), fixed2),
                  pl.BlockSpec((1, LANES), fixed2)],
        out_specs=(pl.BlockSpec((step, d), tok),
                   pl.BlockSpec((step, d // 2 + LANES), tok),
                   pl.BlockSpec((step // ts, SUBLANES, ts), lambda b, s: (b * seq_steps + s, 0, 0))),
        out_shape=_route_out(t, d, ts),
        scratch_shapes=[pltpu.VMEM((POOL_HALO, d), F32)],
        compiler_params=_params("arbitrary", "arbitrary"),
        name="pool_route",
    )(x2, gmix, pw, pscale, gffn, wr, br)


def _positions_kernel(cls_ref, pos_ref, tiles_ref, *, tm):
    n_t, _, ts = cls_ref.shape
    ci = lax.broadcasted_iota(jnp.int32, (CLASS_ROWS, ts), 0).astype(F32)

    def onehot(i):
        return (cls_ref[i, 0:1, :] == ci).astype(F32)

    acc = lax.fori_loop(0, n_t, lambda i, a: a + onehot(i), jnp.zeros((CLASS_ROWS, ts), F32))
    counts = jnp.sum(acc, axis=1, keepdims=True)
    tiles = jnp.floor((counts + (tm - 1)) * (1.0 / tm))
    tiles_b = jnp.broadcast_to(tiles, (CLASS_ROWS, LANES))
    tiles_ref[...] = tiles_b
    rr = lax.broadcasted_iota(jnp.int32, (CLASS_ROWS, CLASS_ROWS), 0)
    cc = lax.broadcasted_iota(jnp.int32, (CLASS_ROWS, CLASS_ROWS), 1)
    before = (cc < rr).astype(BF16)
    start = jnp.dot(before, tiles_b.astype(BF16), preferred_element_type=F32)[:, 0:1] * tm
    jj = lax.broadcasted_iota(jnp.int32, (ts, ts), 0)
    ss = lax.broadcasted_iota(jnp.int32, (ts, ts), 1)
    earlier = (jj < ss).astype(BF16)

    def body(i, run):
        oh = onehot(i)
        rank = jnp.dot(oh.astype(BF16), earlier, preferred_element_type=F32)
        pos_ref[i] = jnp.sum(oh * (run + rank), axis=0, keepdims=True).astype(jnp.int32)
        return run + jnp.sum(oh, axis=1, keepdims=True)

    lax.fori_loop(0, n_t, body, start)


def _positions(cls3, tm):
    n_t, _, ts = cls3.shape
    return pl.pallas_call(
        functools.partial(_positions_kernel, tm=tm),
        out_shape=(jax.ShapeDtypeStruct((n_t, 1, ts), jnp.int32),
                   jax.ShapeDtypeStruct((CLASS_ROWS, LANES), F32)),
        compiler_params=pltpu.CompilerParams(vmem_limit_bytes=VMEM_LIMIT),
        name="positions",
    )(cls3)


def _start_rows(n, make):
    def start(r, _):
        make(r).start()
        return 0

    lax.fori_loop(0, n, start, 0, unroll=DMA_UNROLL)


def _dispatch_kernel(tail_ref, pos_ref, xr_ref, xs_hbm, rows_ref, zero_ref, sems, zsem, *, tm, n_tiles):
    i = pl.program_id(0)
    n_steps = pl.num_programs(0)
    td = xr_ref.shape[0]
    n_cls = tail_ref.shape[0] - 1
    slot = i % 2

    @pl.when(i == 0)
    def _():
        zero_ref[...] = jnp.zeros_like(zero_ref)
        n_act = tail_ref[n_cls]

        def zero_tile(start):
            return pltpu.make_async_copy(zero_ref, xs_hbm.at[pl.ds(pl.multiple_of(start, tm), tm)], zsem)

        todo = [(tail_ref[c] >= 0, jnp.maximum(tail_ref[c], 0)) for c in range(n_cls)]
        todo += [(n_act + c < n_tiles, jnp.minimum(n_act + c, n_tiles - 1) * tm) for c in range(n_cls)]
        for needed, start in todo:
            @pl.when(needed)
            def _():
                zero_tile(start).start()
        for needed, start in todo:
            @pl.when(needed)
            def _():
                zero_tile(start).wait()

    def wait_rows(s):
        pltpu.make_async_copy(rows_ref.at[s], xs_hbm.at[pl.ds(0, td)], sems.at[s]).wait()

    rows_ref[slot] = xr_ref[...]
    _start_rows(td, lambda r: pltpu.make_async_copy(
        rows_ref.at[slot, pl.ds(r, 1)], xs_hbm.at[pl.ds(pos_ref[0, 0, r], 1)], sems.at[slot]))

    @pl.when(i > 0)
    def _():
        wait_rows(1 - slot)

    @pl.when(i == n_steps - 1)
    def _():
        wait_rows(slot)


def _dispatch(tail, pos3, xr, n_tiles, tm):
    t, dw = xr.shape
    td = pos3.shape[2]
    return pl.pallas_call(
        functools.partial(_dispatch_kernel, tm=tm, n_tiles=n_tiles),
        grid_spec=pltpu.PrefetchScalarGridSpec(
            num_scalar_prefetch=1,
            grid=(t // td,),
            in_specs=[pl.BlockSpec((1, 1, td), lambda i, tail: (i, 0, 0), memory_space=pltpu.SMEM),
                      pl.BlockSpec((td, dw), lambda i, tail: (i, 0))],
            out_specs=pl.BlockSpec(memory_space=pl.ANY),
            scratch_shapes=[pltpu.VMEM((2, td, dw), WORD),
                            pltpu.VMEM((tm, dw), WORD),
                            pltpu.SemaphoreType.DMA((2,)),
                            pltpu.SemaphoreType.DMA(())]),
        out_shape=jax.ShapeDtypeStruct((n_tiles * tm, dw), WORD),
        compiler_params=_params("arbitrary"),
        name="dispatch",
    )(tail, pos3, xr)


def _moe_kernel(elo_ref, ehi_ref, nact_ref, xs_ref, wg_lo, wg_hi, wu_lo, wu_hi, wd_lo, wd_hi, ys_ref):
    half = ys_ref.shape[1]
    active = pl.program_id(0) < nact_ref[0]

    @pl.when(active)
    def _():
        x_hi, x_lo = _unpack_pairs(xs_ref[:, :half])
        x = jnp.concatenate([x_hi.astype(BF16), x_lo.astype(BF16)], axis=1)

        def expert(wg, wu, lane):
            g = jnp.dot(x, wg[0], preferred_element_type=F32)
            u = jnp.dot(x, wu[0], preferred_element_type=F32)
            gate = lax.bitcast_convert_type(xs_ref[:, half + lane:half + lane + 1], F32)
            return (g * (1.0 / (1.0 + jnp.exp(-g))) * u * gate).astype(BF16)

        y = (jnp.dot(expert(wg_lo, wu_lo, 1), wd_lo[0], preferred_element_type=F32)
             + jnp.dot(expert(wg_hi, wu_hi, 2), wd_hi[0], preferred_element_type=F32))
        ys_ref[...] = _pack_pairs(y[:, :half], y[:, half:])

    @pl.when(jnp.logical_not(active))
    def _():
        ys_ref[...] = jnp.zeros_like(ys_ref)


def _moe(e_lo, e_hi, n_act, xs, wg, wu, wd, tm):
    n_rows, dw = xs.shape
    half = dw - LANES
    d = 2 * half
    f = wg.shape[2]
    tile = lambda i, lo, hi, na: (jnp.minimum(i, na[0] - 1), 0)
    w_lo = lambda i, lo, hi, na: (lo[i], 0, 0)
    w_hi = lambda i, lo, hi, na: (hi[i], 0, 0)
    return pl.pallas_call(
        _moe_kernel,
        grid_spec=pltpu.PrefetchScalarGridSpec(
            num_scalar_prefetch=3,
            grid=(n_rows // tm,),
            in_specs=[pl.BlockSpec((tm, dw), tile),
                      pl.BlockSpec((1, d, f), w_lo), pl.BlockSpec((1, d, f), w_hi),
                      pl.BlockSpec((1, d, f), w_lo), pl.BlockSpec((1, d, f), w_hi),
                      pl.BlockSpec((1, f, d), w_lo), pl.BlockSpec((1, f, d), w_hi)],
            out_specs=pl.BlockSpec((tm, half), lambda i, lo, hi, na: (i, 0))),
        out_shape=jax.ShapeDtypeStruct((n_rows, half), WORD),
        compiler_params=_params("arbitrary"),
        name="moe",
    )(e_lo, e_hi, n_act, xs, wg, wg, wu, wu, wd, wd)


def _combine_kernel(pos_ref, posn_ref, h_ref, ys_hbm, out_ref, rows_ref, sems):
    i = pl.program_id(0)
    n_steps = pl.num_programs(0)
    tc = h_ref.shape[0]
    slot = i % 2

    def gather(p_ref, s):
        _start_rows(tc, lambda r: pltpu.make_async_copy(
            ys_hbm.at[pl.ds(p_ref[0, 0, r], 1)], rows_ref.at[s, pl.ds(r, 1)], sems.at[s]))

    @pl.when(i == 0)
    def _():
        gather(pos_ref, 0)

    @pl.when(i + 1 < n_steps)
    def _():
        gather(posn_ref, 1 - slot)

    pltpu.make_async_copy(ys_hbm.at[pl.ds(0, tc)], rows_ref.at[slot], sems.at[slot]).wait()
    y_hi, y_lo = _unpack_pairs(rows_ref[slot])
    out_ref[...] = h_ref[...] + jnp.concatenate([y_hi, y_lo], axis=1)


def _combine(pos3, h, ys):
    t, d = h.shape
    n_steps, _, tc = pos3.shape
    return pl.pallas_call(
        _combine_kernel,
        grid=(n_steps,),
        in_specs=[pl.BlockSpec((1, 1, tc), lambda i: (i, 0, 0), memory_space=pltpu.SMEM),
                  pl.BlockSpec((1, 1, tc), lambda i: (jnp.minimum(i + 1, n_steps - 1), 0, 0),
                               memory_space=pltpu.SMEM),
                  pl.BlockSpec((tc, d), lambda i: (i, 0)),
                  pl.BlockSpec(memory_space=pl.ANY)],
        out_specs=pl.BlockSpec((tc, d), lambda i: (i, 0)),
        out_shape=jax.ShapeDtypeStruct((t, d), F32),
        scratch_shapes=[pltpu.VMEM((2, tc, d // 2), WORD), pltpu.SemaphoreType.DMA((2,))],
        compiler_params=_params("arbitrary"),
        name="combine",
    )(pos3, pos3, h, ys)


def _kvq_kernel(h_ref, gkv_ref, gq_ref, w_ref, gain_ref, out_ref, xkv_ref, xq_ref, *, d):
    n = pl.program_id(1)
    bn = out_ref.shape[1]
    k_blocks, kv_blocks = d // bn, 2 * d // bn

    @pl.when(n == 0)
    def _():
        h = h_ref[...]
        xhat = h * _rms_scale(h)
        xkv_ref[...] = (xhat * gkv_ref[...]).astype(BF16)
        xq_ref[...] = (xhat * gq_ref[...]).astype(BF16)

    def project(x_ref, head_norm):
        x = x_ref[...]
        for c0 in range(0, bn, MXU_COLS):
            acc = jnp.dot(x, w_ref[:, c0:c0 + MXU_COLS], preferred_element_type=F32)
            if head_norm:
                for hh in range(MXU_COLS // HEAD_DIM):
                    cols = slice(c0 + hh * HEAD_DIM, c0 + (hh + 1) * HEAD_DIM)
                    blk = acc[:, hh * HEAD_DIM:(hh + 1) * HEAD_DIM]
                    out_ref[:, cols] = (blk * _rms_scale(blk) * gain_ref[:, cols]).astype(BF16)
            else:
                out_ref[:, c0:c0 + MXU_COLS] = acc.astype(BF16)

    @pl.when(n < k_blocks)
    def _():
        project(xkv_ref, True)

    @pl.when((n >= k_blocks) & (n < kv_blocks))
    def _():
        project(xkv_ref, False)

    @pl.when(n >= kv_blocks)
    def _():
        project(xq_ref, True)


def _kvq(h, gkv, gq, w_all, gain_all):
    t, d = h.shape
    n_out = w_all.shape[1]
    tm, bn = PROJ_TILE, PROJ_COLS
    return pl.pallas_call(
        functools.partial(_kvq_kernel, d=d),
        grid=(t // tm, n_out // bn),
        in_specs=[pl.BlockSpec((tm, d), lambda i, n: (i, 0)),
                  pl.BlockSpec((1, d), lambda i, n: (0, 0)),
                  pl.BlockSpec((1, d), lambda i, n: (0, 0)),
                  pl.BlockSpec((d, bn), lambda i, n: (0, n)),
                  pl.BlockSpec((1, bn), lambda i, n: (0, n))],
        out_specs=pl.BlockSpec((tm, bn), lambda i, n: (i, n)),
        out_shape=jax.ShapeDtypeStruct((t, n_out), BF16),
        scratch_shapes=[pltpu.VMEM((tm, d), BF16), pltpu.VMEM((tm, d), BF16)],
        compiler_params=_params("arbitrary", "arbitrary"),
        name="kvq",
    )(h, gkv, gq, w_all, gain_all)


def _attn_kernel(q_ref, k_ref, v_ref, o_ref, acc_ref, carry_ref):
    qi = pl.program_id(2)
    tq = q_ref.shape[0]
    tk = ATTN_KEYS
    n_heads = q_ref.shape[1] // HEAD_DIM
    n_before = (qi * tq) // tk
    jj = lax.broadcasted_iota(jnp.int32, (tk, tk), 0)
    ss = lax.broadcasted_iota(jnp.int32, (tk, tk), 1)
    later = (jj > ss).astype(BF16)
    rows = qi * tq + lax.broadcasted_iota(jnp.int32, (tq, tk), 0)
    cols = n_before * tk + lax.broadcasted_iota(jnp.int32, (tq, tk), 1)
    causal = cols < rows

    def block(g, j, first):
        hd = slice(g * HEAD_DIM, (g + 1) * HEAD_DIM)
        start = pl.multiple_of(j * tk, tk)
        z = lax.dot_general(q_ref[:, hd], k_ref[pl.ds(start, tk), hd], (((1,), (1,)), ((), ())),
                            preferred_element_type=F32)
        sp = jnp.maximum(z, 0.0) + jnp.log2(1.0 + jnp.exp2(-jnp.abs(z)))
        if first:
            sp = jnp.where(causal, sp, 0.0)
        after = jnp.dot(sp.astype(BF16), later, preferred_element_type=F32)
        logit = z - sp - after
        if not first:
            c = carry_ref[g]
            logit = logit - jnp.concatenate([c] * (tk // LANES), axis=1)
        a = jnp.exp2(logit)
        if first:
            a = jnp.where(causal, a, 0.0)
        pv = jnp.dot(a.astype(BF16), v_ref[pl.ds(start, tk), hd], preferred_element_type=F32)
        mass = jnp.sum(sp, axis=1, keepdims=True)
        if first:
            acc_ref[g] = pv
            carry_ref[g] = jnp.broadcast_to(mass, (tq, LANES))
        else:
            acc_ref[g] += pv
            carry_ref[g] += mass

    for g in range(n_heads):
        block(g, n_before, True)

    def body(i, _):
        for g in range(n_heads):
            block(g, n_before - 1 - i, False)
        return 0

    lax.fori_loop(0, n_before, body, 0)
    for g in range(n_heads):
        o_ref[:, g * HEAD_DIM:(g + 1) * HEAD_DIM] = acc_ref[g].astype(BF16)


def _attention(kvq, batch, d):
    t = kvq.shape[0]
    seq = t // batch
    gw = ATTN_HEADS * HEAD_DIM
    n_hg = d // gw
    tq = ATTN_TILE
    nq = seq // tq
    return pl.pallas_call(
        _attn_kernel,
        grid=(batch, n_hg, nq),
        in_specs=[pl.BlockSpec((tq, gw), lambda b, h, i: (b * nq + i, 2 * n_hg + h)),
                  pl.BlockSpec((seq, gw), lambda b, h, i: (b, h)),
                  pl.BlockSpec((seq, gw), lambda b, h, i: (b, n_hg + h))],
        out_specs=pl.BlockSpec((tq, gw), lambda b, h, i: (b * nq + i, h)),
        out_shape=jax.ShapeDtypeStruct((t, d), BF16),
        scratch_shapes=[pltpu.VMEM((ATTN_HEADS, tq, HEAD_DIM), F32),
                        pltpu.VMEM((ATTN_HEADS, tq, LANES), F32)],
        compiler_params=_params("arbitrary", "arbitrary", "arbitrary"),
        name="attention",
    )(kvq, kvq, kvq)


def _oproj_route_kernel(o_ref, wo_ref, h_ref, gffn_ref, wr_ref, br_ref,
                        hout_ref, xr_ref, cls_ref, *, n_grp, n_exp):
    ts = TOKEN_TILE
    for sub in range(o_ref.shape[0] // ts):
        rows = pl.ds(sub * ts, ts)
        h = h_ref[rows, :] + jnp.dot(o_ref[rows, :], wo_ref[...], preferred_element_type=F32)
        hout_ref[rows, :] = h
        _ffn_norm_route(h, gffn_ref, wr_ref, br_ref, xr_ref, cls_ref, sub, n_grp, n_exp)


def _oproj_route(o, wo, h, gffn, wr, br, n_grp, n_exp):
    t, d = h.shape
    ts, step = TOKEN_TILE, ROUTE_STEP
    tok = lambda i: (i, 0)
    fixed2 = lambda i: (0, 0)
    return pl.pallas_call(
        functools.partial(_oproj_route_kernel, n_grp=n_grp, n_exp=n_exp),
        grid=(t // step,),
        in_specs=[pl.BlockSpec((step, d), tok),
                  pl.BlockSpec((d, d), fixed2),
                  pl.BlockSpec((step, d), tok),
                  pl.BlockSpec((1, d), fixed2),
                  pl.BlockSpec((d, LANES), fixed2),
                  pl.BlockSpec((1, LANES), fixed2)],
        out_specs=(pl.BlockSpec((step, d), tok),
                   pl.BlockSpec((step, d // 2 + LANES), tok),
                   pl.BlockSpec((step // ts, SUBLANES, ts), lambda i: (i, 0, 0))),
        out_shape=_route_out(t, d, ts),
        compiler_params=_params("arbitrary"),
        name="oproj_route",
    )(o, wo, h, gffn, wr, br)


def _router_weights(w_grp, b_grp, w_rt, b_rt):
    d, n_grp = w_grp.shape
    n_exp = w_rt.shape[2]
    w = jnp.concatenate([w_grp, jnp.transpose(w_rt, (1, 0, 2)).reshape(d, n_grp * n_exp)], axis=1)
    b = jnp.concatenate([b_grp, b_rt.reshape(-1)])
    pad = LANES - w.shape[1]
    return (jnp.pad(w, ((0, 0), (0, pad))).astype(BF16), jnp.pad(b, (0, pad)).reshape(1, LANES))


def _pair_table(n_grp, n_exp):
    lo, hi = [], []
    for g in range(n_grp):
        for a in range(n_exp):
            for b in range(a + 1, n_exp):
                lo.append(g * n_exp + a)
                hi.append(g * n_exp + b)
    return jnp.array(lo, jnp.int32), jnp.array(hi, jnp.int32)


def _moe_layer(h, xr, cls3, wg, wu, wd, n_grp, n_exp):
    t, d = h.shape
    tm = MOE_TILE
    cls_lo, cls_hi = _pair_table(n_grp, n_exp)
    n_cls = cls_lo.shape[0]
    n_tiles = t // tm + n_cls
    assert n_cls <= CLASS_ROWS and n_tiles < 256
    pos3, tiles = _positions(cls3, tm)
    tiles_c = tiles[:n_cls, 0].astype(jnp.int32)
    ends = jnp.cumsum(tiles_c)
    n_act = ends[-1]
    tail = jnp.where(tiles_c > 0, (ends - 1) * tm, -1).astype(jnp.int32)
    tile_ids = jnp.minimum(jnp.arange(n_tiles, dtype=jnp.int32), n_act - 1)
    tile_cls = jnp.sum(tile_ids[:, None] >= ends[None, :], axis=1)
    xs = _dispatch(jnp.concatenate([tail, n_act.reshape(1)]), pos3, xr, n_tiles, tm)
    ys = _moe(cls_lo[tile_cls], cls_hi[tile_cls], n_act.reshape(1), xs, wg, wu, wd, tm)
    return _combine(pos3, h, ys)


def kernel(x, norm_mix, norm_ffn, pool_w, pool_scale, kv_norm, w_kv, k_norm, w_q, q_norm, w_o,
           moe_w_grp, moe_b_grp, moe_w_rt, moe_b_rt, moe_w_gate, moe_w_up, moe_w_down):
    batch, seq, d = x.shape
    t = batch * seq
    n_grp, n_exp = moe_w_rt.shape[1], moe_w_rt.shape[3]
    n_heads = d // HEAD_DIM
    assert len(POOL_WINDOWS) == pool_w.shape[1] and seq % ROUTE_STEP == 0 and seq % ATTN_TILE == 0
    assert t % PROJ_TILE == 0 and d % PROJ_COLS == 0 and n_grp + n_grp * n_exp <= LANES
    assert n_heads % ATTN_HEADS == 0 and seq % ATTN_KEYS == 0 and ATTN_KEYS % ATTN_TILE == 0
    row = lambda v: v.reshape(1, -1).astype(F32)

    wr0, br0 = _router_weights(moe_w_grp[0], moe_b_grp[0], moe_w_rt[0], moe_b_rt[0])
    h1, xr, cls3 = _pool_route(x.reshape(t, d), row(norm_mix[0]), pool_w[0].astype(BF16),
                               row(pool_scale[0]), row(norm_ffn[0]), wr0, br0, batch, n_grp, n_exp)
    h2 = _moe_layer(h1, xr, cls3, moe_w_gate[0].astype(BF16), moe_w_up[0].astype(BF16),
                    moe_w_down[0].astype(BF16), n_grp, n_exp)

    w_all = jnp.concatenate([w_kv, w_q[0]], axis=1).astype(BF16)
    q_gain = q_norm[0] * (HEAD_DIM ** -0.5 * LOG2E)
    gain_all = jnp.concatenate([jnp.tile(k_norm, n_heads), jnp.ones((d,), F32), jnp.tile(q_gain, n_heads)])
    kvq = _kvq(h2, row(kv_norm), row(norm_mix[1]), w_all, row(gain_all))
    o = _attention(kvq, batch, d)

    wr1, br1 = _router_weights(moe_w_grp[1], moe_b_grp[1], moe_w_rt[1], moe_b_rt[1])
    h3, xr, cls3 = _oproj_route(o, w_o[0].astype(BF16), h2, row(norm_ffn[1]), wr1, br1, n_grp, n_exp)
    h4 = _moe_layer(h3, xr, cls3, moe_w_gate[1].astype(BF16), moe_w_up[1].astype(BF16),
                    moe_w_down[1].astype(BF16), n_grp, n_exp)
    return h4.reshape(batch, seq, d)
```

```python
import functools

import jax
import jax.numpy as jnp
from jax import lax
from jax.experimental import pallas as pl
from jax.experimental.pallas import tpu as pltpu

F32 = jnp.float32
BF16 = jnp.bfloat16
WORD = jnp.uint32

EPS = 1e-6
POOL_WINDOWS = (2, 4, 8, 16)
POOL_HALO = 16
HEAD_DIM = 128
LANES = 128
SUBLANES = 8
MXU_COLS = 256
LOG2E = 1.4426950408889634

TOKEN_TILE = 256
ROUTE_STEP = 512
MOE_TILE = 256
PROJ_TILE = 1024
PROJ_COLS = 1024
ATTN_TILE = 256
ATTN_KEYS = 256
ATTN_HEADS = 8
CLASS_ROWS = 32
VMEM_LIMIT = 56 * 1024 * 1024


def _params(*sem):
    return pltpu.CompilerParams(dimension_semantics=sem, vmem_limit_bytes=VMEM_LIMIT)


def _rms_scale(x):
    return lax.rsqrt(jnp.mean(x * x, axis=-1, keepdims=True) + EPS)


def _pack_pairs(first, second):
    return pltpu.pack_elementwise([first, second], packed_dtype=BF16)


def _unpack_pairs(words):
    return tuple(pltpu.unpack_elementwise(words, index=i, packed_dtype=BF16, unpacked_dtype=F32)
                 for i in range(2))


def _first_argmax(vals):
    best, idx = vals[0], jnp.zeros(vals[0].shape, jnp.int32)
    for i in range(1, len(vals)):
        upd = vals[i] > best
        best = jnp.where(upd, vals[i], best)
        idx = jnp.where(upd, i, idx)
    return best, idx


def _ffn_norm_route(h, gffn_ref, wr_ref, br_ref, xr_ref, cls_ref, sub, n_grp, n_exp):
    ts, d = h.shape
    half = d // 2
    rows = pl.ds(sub * ts, ts)
    xn = h * _rms_scale(h) * gffn_ref[...]
    xr_ref[rows, :half] = _pack_pairs(xn[:, :half], xn[:, half:])
    logits = jnp.dot(xn.astype(BF16), wr_ref[...], preferred_element_type=F32) + br_ref[...]
    lt = logits.T
    row = lambda i: lt[i:i + 1, :]
    g_best, g_idx = _first_argmax([row(g) for g in range(n_grp)])
    denom = sum(jnp.exp(row(g) - g_best) for g in range(n_grp))
    g_w = 1.0 / denom
    sel = []
    for e in range(n_exp):
        v = row(n_grp + e)
        for g in range(1, n_grp):
            v = jnp.where(g_idx == g, row(n_grp + g * n_exp + e), v)
        sel.append(v)
    v1, i1 = _first_argmax(sel)
    v2 = jnp.full_like(v1, -jnp.inf)
    i2 = jnp.zeros_like(i1)
    for e in range(n_exp):
        upd = (i1 != e) & (sel[e] > v2)
        v2 = jnp.where(upd, sel[e], v2)
        i2 = jnp.where(upd, e, i2)
    t = jnp.exp(v2 - v1)
    w1 = g_w / (1.0 + t)
    w2 = w1 * t
    first_low = i1 < i2
    lo = jnp.where(first_low, i1, i2)
    hi = jnp.where(first_low, i2, i1)
    w_lo = jnp.where(first_low, w1, w2)
    w_hi = jnp.where(first_low, w2, w1)
    n_pairs = n_exp * (n_exp - 1) // 2
    pair = ((lo * (2 * n_exp - 1 - lo)) >> 1) + (hi - lo - 1)
    cls = (g_idx * n_pairs + pair).astype(F32)
    r = lax.broadcasted_iota(jnp.int32, (LANES, ts), 0)
    packed = jnp.where(r == 0, cls, jnp.where(r == 1, w_lo, jnp.where(r == 2, w_hi, 0.0)))
    cls_ref[sub] = packed[:SUBLANES, :]
    xr_ref[rows, half:] = lax.bitcast_convert_type(packed.T, xr_ref.dtype)


def _route_out(t, d, ts):
    return (jax.ShapeDtypeStruct((t, d), F32),
            jax.ShapeDtypeStruct((t, d // 2 + LANES), WORD),
            jax.ShapeDtypeStruct((t // ts, SUBLANES, ts), F32))


def _pool_route_kernel(x_ref, gmix_ref, pw_ref, pscale_ref, gffn_ref, wr_ref, br_ref,
                       h_ref, xr_ref, cls_ref, halo_ref, *, n_grp, n_exp):
    s = pl.program_id(1)
    d = x_ref.shape[1]
    ts = TOKEN_TILE
    c = d // len(POOL_WINDOWS)

    @pl.when(s == 0)
    def _():
        halo_ref[...] = jnp.zeros_like(halo_ref)

    for sub in range(x_ref.shape[0] // ts):
        rows = pl.ds(sub * ts, ts)
        x = x_ref[rows, :]
        xn = x * _rms_scale(x) * gmix_ref[...]
        ext = jnp.concatenate([halo_ref[...], xn], axis=0)
        halo_ref[...] = xn[ts - POOL_HALO:, :]
        pos = (s * x_ref.shape[0] + sub * ts) + lax.broadcasted_iota(jnp.int32, (ts, LANES), 0)
        for gi, w in enumerate(POOL_WINDOWS):
            acc = ext[:, gi * c:(gi + 1) * c]
            k = 1
            while k < w:
                acc = acc + pltpu.roll(acc, k, 0)
                k *= 2
            inv_cnt = 1.0 / jnp.minimum(pos + 1, w).astype(F32)
            inv_cnt = jnp.concatenate([inv_cnt] * (c // LANES), axis=1)
            pooled = acc[POOL_HALO:, :] * inv_cnt - xn[:, gi * c:(gi + 1) * c]
            y = jnp.dot(pooled.astype(BF16), pw_ref[gi], preferred_element_type=F32)
            h_ref[rows, gi * c:(gi + 1) * c] = (x[:, gi * c:(gi + 1) * c]
                                                + y * pscale_ref[:, gi * c:(gi + 1) * c])
        _ffn_norm_route(h_ref[rows, :], gffn_ref, wr_ref, br_ref, xr_ref, cls_ref, sub, n_grp, n_exp)


def _pool_route(x2, gmix, pw, pscale, gffn, wr, br, batch, n_grp, n_exp):
    t, d = x2.shape
    ts, step = TOKEN_TILE, ROUTE_STEP
    seq_steps = t // batch // step
    tok = lambda b, s: (b * seq_steps + s, 0)
    fixed2 = lambda b, s: (0, 0)
    return pl.pallas_call(
        functools.partial(_pool_route_kernel, n_grp=n_grp, n_exp=n_exp),
        grid=(batch, seq_steps),
        in_specs=[pl.BlockSpec((step, d), tok),
                  pl.BlockSpec((1, d), fixed2),
                  pl.BlockSpec(pw.shape, lambda b, s: (0, 0, 0)),
                  pl.BlockSpec((1, d), fixed2),
                  pl.BlockSpec((1, d), fixed2),
                  pl.BlockSpec((d, LANES), fixed2),
                  pl.BlockSpec((1, LANES), fixed2)],
        out_specs=(pl.BlockSpec((step, d), tok),
                   pl.BlockSpec((step, d // 2 + LANES), tok),
                   pl.BlockSpec((step // ts, SUBLANES, ts), lambda b, s: (b * seq_steps + s, 0, 0))),
        out_shape=_route_out(t, d, ts),
        scratch_shapes=[pltpu.VMEM((POOL_HALO, d), F32)],
        compiler_params=_params("arbitrary", "arbitrary"),
        name="pool_route",
    )(x2, gmix, pw, pscale, gffn, wr, br)


def _positions_kernel(cls_ref, pos_ref, tiles_ref, *, tm):
    n_t, _, ts = cls_ref.shape
    ci = lax.broadcasted_iota(jnp.int32, (CLASS_ROWS, ts), 0).astype(F32)

    def onehot(i):
        return (cls_ref[i, 0:1, :] == ci).astype(F32)

    acc = lax.fori_loop(0, n_t, lambda i, a: a + onehot(i), jnp.zeros((CLASS_ROWS, ts), F32))
    counts = jnp.sum(acc, axis=1, keepdims=True)
    tiles = jnp.floor((counts + (tm - 1)) * (1.0 / tm))
    tiles_b = jnp.broadcast_to(tiles, (CLASS_ROWS, LANES))
    tiles_ref[...] = tiles_b
    rr = lax.broadcasted_iota(jnp.int32, (CLASS_ROWS, CLASS_ROWS), 0)
    cc = lax.broadcasted_iota(jnp.int32, (CLASS_ROWS, CLASS_ROWS), 1)
    before = (cc < rr).astype(BF16)
    start = jnp.dot(before, tiles_b.astype(BF16), preferred_element_type=F32)[:, 0:1] * tm
    jj = lax.broadcasted_iota(jnp.int32, (ts, ts), 0)
    ss = lax.broadcasted_iota(jnp.int32, (ts, ts), 1)
    earlier = (jj < ss).astype(BF16)

    def body(i, run):
        oh = onehot(i)
        rank = jnp.dot(oh.astype(BF16), earlier, preferred_element_type=F32)
        pos_ref[i] = jnp.sum(oh * (run + rank), axis=0, keepdims=True).astype(jnp.int32)
        return run + jnp.sum(oh, axis=1, keepdims=True)

    lax.fori_loop(0, n_t, body, start)


def _positions(cls3, tm):
    n_t, _, ts = cls3.shape
    return pl.pallas_call(
        functools.partial(_positions_kernel, tm=tm),
        out_shape=(jax.ShapeDtypeStruct((n_t, 1, ts), jnp.int32),
                   jax.ShapeDtypeStruct((CLASS_ROWS, LANES), F32)),
        compiler_params=pltpu.CompilerParams(vmem_limit_bytes=VMEM_LIMIT),
        name="positions",
    )(cls3)


def _start_rows(n, make):
    for r in range(n):
        make(r).start()


def _dispatch_kernel(tail_ref, pos_ref, xr_ref, xs_hbm, rows_ref, zero_ref, sems, zsem, *, tm, n_tiles):
    i = pl.program_id(0)
    n_steps = pl.num_programs(0)
    td = xr_ref.shape[0]
    n_cls = tail_ref.shape[0] - 1
    slot = i % 2

    @pl.when(i == 0)
    def _():
        zero_ref[...] = jnp.zeros_like(zero_ref)
        n_act = tail_ref[n_cls]

        def zero_tile(start):
            return pltpu.make_async_copy(zero_ref, xs_hbm.at[pl.ds(pl.multiple_of(start, tm), tm)], zsem)

        todo = [(tail_ref[c] >= 0, jnp.maximum(tail_ref[c], 0)) for c in range(n_cls)]
        todo += [(n_act + c < n_tiles, jnp.minimum(n_act + c, n_tiles - 1) * tm) for c in range(n_cls)]
        for needed, start in todo:
            @pl.when(needed)
            def _():
                zero_tile(start).start()
        for needed, start in todo:
            @pl.when(needed)
            def _():
                zero_tile(start).wait()

    def wait_rows(s):
        pltpu.make_async_copy(rows_ref.at[s], xs_hbm.at[pl.ds(0, td)], sems.at[s]).wait()

    rows_ref[slot] = xr_ref[...]
    _start_rows(td, lambda r: pltpu.make_async_copy(
        rows_ref.at[slot, pl.ds(r, 1)], xs_hbm.at[pl.ds(pos_ref[0, 0, r], 1)], sems.at[slot]))

    @pl.when(i > 0)
    def _():
        wait_rows(1 - slot)

    @pl.when(i == n_steps - 1)
    def _():
        wait_rows(slot)


def _dispatch(tail, pos3, xr, n_tiles, tm):
    t, dw = xr.shape
    td = pos3.shape[2]
    return pl.pallas_call(
        functools.partial(_dispatch_kernel, tm=tm, n_tiles=n_tiles),
        grid_spec=pltpu.PrefetchScalarGridSpec(
            num_scalar_prefetch=1,
            grid=(t // td,),
            in_specs=[pl.BlockSpec((1, 1, td), lambda i, tail: (i, 0, 0), memory_space=pltpu.SMEM),
                      pl.BlockSpec((td, dw), lambda i, tail: (i, 0))],
            out_specs=pl.BlockSpec(memory_space=pl.ANY),
            scratch_shapes=[pltpu.VMEM((2, td, dw), WORD),
                            pltpu.VMEM((tm, dw), WORD),
                            pltpu.SemaphoreType.DMA((2,)),
                            pltpu.SemaphoreType.DMA(())]),
        out_shape=jax.ShapeDtypeStruct((n_tiles * tm, dw), WORD),
        compiler_params=_params("arbitrary"),
        name="dispatch",
    )(tail, pos3, xr)


def _moe_kernel(elo_ref, ehi_ref, nact_ref, xs_ref, wg_lo, wg_hi, wu_lo, wu_hi, wd_lo, wd_hi, ys_ref):
    half = ys_ref.shape[1]
    active = pl.program_id(0) < nact_ref[0]

    @pl.when(active)
    def _():
        x_hi, x_lo = _unpack_pairs(xs_ref[:, :half])
        x = jnp.concatenate([x_hi.astype(BF16), x_lo.astype(BF16)], axis=1)

        def expert(wg, wu, lane):
            g = jnp.dot(x, wg[0], preferred_element_type=F32)
            u = jnp.dot(x, wu[0], preferred_element_type=F32)
            gate = lax.bitcast_convert_type(xs_ref[:, half + lane:half + lane + 1], F32)
            return (g * (1.0 / (1.0 + jnp.exp(-g))) * u * gate).astype(BF16)

        y = (jnp.dot(expert(wg_lo, wu_lo, 1), wd_lo[0], preferred_element_type=F32)
             + jnp.dot(expert(wg_hi, wu_hi, 2), wd_hi[0], preferred_element_type=F32))
        ys_ref[...] = _pack_pairs(y[:, :half], y[:, half:])

    @pl.when(jnp.logical_not(active))
    def _():
        ys_ref[...] = jnp.zeros_like(ys_ref)


def _moe(e_lo, e_hi, n_act, xs, wg, wu, wd, tm):
    n_rows, dw = xs.shape
    half = dw - LANES
    d = 2 * half
    f = wg.shape[2]
    tile = lambda i, lo, hi, na: (jnp.minimum(i, na[0] - 1), 0)
    w_lo = lambda i, lo, hi, na: (lo[i], 0, 0)
    w_hi = lambda i, lo, hi, na: (hi[i], 0, 0)
    return pl.pallas_call(
        _moe_kernel,
        grid_spec=pltpu.PrefetchScalarGridSpec(
            num_scalar_prefetch=3,
            grid=(n_rows // tm,),
            in_specs=[pl.BlockSpec((tm, dw), tile),
                      pl.BlockSpec((1, d, f), w_lo), pl.BlockSpec((1, d, f), w_hi),
                      pl.BlockSpec((1, d, f), w_lo), pl.BlockSpec((1, d, f), w_hi),
                      pl.BlockSpec((1, f, d), w_lo), pl.BlockSpec((1, f, d), w_hi)],
            out_specs=pl.BlockSpec((tm, half), lambda i, lo, hi, na: (i, 0))),
        out_shape=jax.ShapeDtypeStruct((n_rows, half), WORD),
        compiler_params=_params("arbitrary"),
        name="moe",
    )(e_lo, e_hi, n_act, xs, wg, wg, wu, wu, wd, wd)


def _combine_kernel(pos_ref, posn_ref, h_ref, ys_hbm, out_ref, rows_ref, sems):
    i = pl.program_id(0)
    n_steps = pl.num_programs(0)
    tc = h_ref.shape[0]
    slot = i % 2

    def gather(p_ref, s):
        _start_rows(tc, lambda r: pltpu.make_async_copy(
            ys_hbm.at[pl.ds(p_ref[0, 0, r], 1)], rows_ref.at[s, pl.ds(r, 1)], sems.at[s]))

    @pl.when(i == 0)
    def _():
        gather(pos_ref, 0)

    @pl.when(i + 1 < n_steps)
    def _():
        gather(posn_ref, 1 - slot)

    pltpu.make_async_copy(ys_hbm.at[pl.ds(0, tc)], rows_ref.at[slot], sems.at[slot]).wait()
    y_hi, y_lo = _unpack_pairs(rows_ref[slot])
    out_ref[...] = h_ref[...] + jnp.concatenate([y_hi, y_lo], axis=1)


def _combine(pos3, h, ys):
    t, d = h.shape
    n_steps, _, tc = pos3.shape
    return pl.pallas_call(
        _combine_kernel,
        grid=(n_steps,),
        in_specs=[pl.BlockSpec((1, 1, tc), lambda i: (i, 0, 0), memory_space=pltpu.SMEM),
                  pl.BlockSpec((1, 1, tc), lambda i: (jnp.minimum(i + 1, n_steps - 1), 0, 0),
                               memory_space=pltpu.SMEM),
                  pl.BlockSpec((tc, d), lambda i: (i, 0)),
                  pl.BlockSpec(memory_space=pl.ANY)],
        out_specs=pl.BlockSpec((tc, d), lambda i: (i, 0)),
        out_shape=jax.ShapeDtypeStruct((t, d), F32),
        scratch_shapes=[pltpu.VMEM((2, tc, d // 2), WORD), pltpu.SemaphoreType.DMA((2,))],
        compiler_params=_params("arbitrary"),
        name="combine",
    )(pos3, pos3, h, ys)


def _kvq_kernel(h_ref, gkv_ref, gq_ref, w_ref, gain_ref, out_ref, xkv_ref, xq_ref, *, d):
    n = pl.program_id(1)
    bn = out_ref.shape[1]
    k_blocks, kv_blocks = d // bn, 2 * d // bn

    @pl.when(n == 0)
    def _():
        h = h_ref[...]
        xhat = h * _rms_scale(h)
        xkv_ref[...] = (xhat * gkv_ref[...]).astype(BF16)
        xq_ref[...] = (xhat * gq_ref[...]).astype(BF16)

    def project(x_ref, head_norm):
        x = x_ref[...]
        for c0 in range(0, bn, MXU_COLS):
            acc = jnp.dot(x, w_ref[:, c0:c0 + MXU_COLS], preferred_element_type=F32)
            if head_norm:
                for hh in range(MXU_COLS // HEAD_DIM):
                    cols = slice(c0 + hh * HEAD_DIM, c0 + (hh + 1) * HEAD_DIM)
                    blk = acc[:, hh * HEAD_DIM:(hh + 1) * HEAD_DIM]
                    out_ref[:, cols] = (blk * _rms_scale(blk) * gain_ref[:, cols]).astype(BF16)
            else:
                out_ref[:, c0:c0 + MXU_COLS] = acc.astype(BF16)

    @pl.when(n < k_blocks)
    def _():
        project(xkv_ref, True)

    @pl.when((n >= k_blocks) & (n < kv_blocks))
    def _():
        project(xkv_ref, False)

    @pl.when(n >= kv_blocks)
    def _():
        project(xq_ref, True)


def _kvq(h, gkv, gq, w_all, gain_all):
    t, d = h.shape
    n_out = w_all.shape[1]
    tm, bn = PROJ_TILE, PROJ_COLS
    return pl.pallas_call(
        functools.partial(_kvq_kernel, d=d),
        grid=(t // tm, n_out // bn),
        in_specs=[pl.BlockSpec((tm, d), lambda i, n: (i, 0)),
                  pl.BlockSpec((1, d), lambda i, n: (0, 0)),
                  pl.BlockSpec((1, d), lambda i, n: (0, 0)),
                  pl.BlockSpec((d, bn), lambda i, n: (0, n)),
                  pl.BlockSpec((1, bn), lambda i, n: (0, n))],
        out_specs=pl.BlockSpec((tm, bn), lambda i, n: (i, n)),
        out_shape=jax.ShapeDtypeStruct((t, n_out), BF16),
        scratch_shapes=[pltpu.VMEM((tm, d), BF16), pltpu.VMEM((tm, d), BF16)],
        compiler_params=_params("arbitrary", "arbitrary"),
        name="kvq",
    )(h, gkv, gq, w_all, gain_all)


def _attn_kernel(q_ref, k_ref, v_ref, o_ref, acc_ref, carry_ref, logit_ref, mass_ref):
    qi = pl.program_id(2)
    tq = q_ref.shape[0]
    tk = ATTN_KEYS
    n_heads = q_ref.shape[1] // HEAD_DIM
    n_before = (qi * tq) // tk
    jj = lax.broadcasted_iota(jnp.int32, (tk, tk), 0)
    ss = lax.broadcasted_iota(jnp.int32, (tk, tk), 1)
    later = (jj > ss).astype(BF16)
    rows = qi * tq + lax.broadcasted_iota(jnp.int32, (tq, tk), 0)
    cols = n_before * tk + lax.broadcasted_iota(jnp.int32, (tq, tk), 1)
    causal = cols < rows

    def keys(j):
        return pl.ds(pl.multiple_of(j * tk, tk), tk)

    def score(g, j, diagonal):
        hd = slice(g * HEAD_DIM, (g + 1) * HEAD_DIM)
        z = lax.dot_general(q_ref[:, hd], k_ref[keys(j), hd], (((1,), (1,)), ((), ())),
                            preferred_element_type=F32)
        sp = jnp.maximum(z, 0.0) + jnp.log2(1.0 + jnp.exp2(-jnp.abs(z)))
        if diagonal:
            sp = jnp.where(causal, sp, 0.0)
        after = jnp.dot(sp.astype(BF16), later, preferred_element_type=F32)
        logit = z - sp - after
        if diagonal:
            logit = jnp.where(causal, logit, -jnp.inf)
        logit_ref[g] = logit
        mass_ref[g] = jnp.broadcast_to(jnp.sum(sp, axis=1, keepdims=True), (tq, LANES))

    def accumulate(g, j):
        hd = slice(g * HEAD_DIM, (g + 1) * HEAD_DIM)
        c = carry_ref[g]
        a = jnp.exp2(logit_ref[g] - jnp.concatenate([c] * (tk // LANES), axis=1))
        acc_ref[g] += jnp.dot(a.astype(BF16), v_ref[keys(j), hd], preferred_element_type=F32)
        carry_ref[g] = c + mass_ref[g]

    acc_ref[...] = jnp.zeros_like(acc_ref)
    carry_ref[...] = jnp.zeros_like(carry_ref)
    for g in range(n_heads):
        score(g, n_before, True)

    def body(i, _):
        j = n_before - 1 - i
        for g in range(n_heads):
            accumulate(g, j + 1)
            score(g, j, False)
        return 0

    lax.fori_loop(0, n_before, body, 0)
    for g in range(n_heads):
        accumulate(g, 0)
        o_ref[:, g * HEAD_DIM:(g + 1) * HEAD_DIM] = acc_ref[g].astype(BF16)


def _attention(kvq, batch, d):
    t = kvq.shape[0]
    seq = t // batch
    gw = ATTN_HEADS * HEAD_DIM
    n_hg = d // gw
    tq = ATTN_TILE
    nq = seq // tq
    return pl.pallas_call(
        _attn_kernel,
        grid=(batch, n_hg, nq),
        in_specs=[pl.BlockSpec((tq, gw), lambda b, h, i: (b * nq + i, 2 * n_hg + h)),
                  pl.BlockSpec((seq, gw), lambda b, h, i: (b, h)),
                  pl.BlockSpec((seq, gw), lambda b, h, i: (b, n_hg + h))],
        out_specs=pl.BlockSpec((tq, gw), lambda b, h, i: (b * nq + i, h)),
        out_shape=jax.ShapeDtypeStruct((t, d), BF16),
        scratch_shapes=[pltpu.VMEM((ATTN_HEADS, tq, HEAD_DIM), F32),
                        pltpu.VMEM((ATTN_HEADS, tq, LANES), F32),
                        pltpu.VMEM((ATTN_HEADS, tq, ATTN_KEYS), F32),
                        pltpu.VMEM((ATTN_HEADS, tq, LANES), F32)],
        compiler_params=_params("arbitrary", "arbitrary", "arbitrary"),
        name="attention",
    )(kvq, kvq, kvq)


def _oproj_route_kernel(o_ref, wo_ref, h_ref, gffn_ref, wr_ref, br_ref,
                        hout_ref, xr_ref, cls_ref, *, n_grp, n_exp):
    ts = TOKEN_TILE
    for sub in range(o_ref.shape[0] // ts):
        rows = pl.ds(sub * ts, ts)
        h = h_ref[rows, :] + jnp.dot(o_ref[rows, :], wo_ref[...], preferred_element_type=F32)
        hout_ref[rows, :] = h
        _ffn_norm_route(h, gffn_ref, wr_ref, br_ref, xr_ref, cls_ref, sub, n_grp, n_exp)


def _oproj_route(o, wo, h, gffn, wr, br, n_grp, n_exp):
    t, d = h.shape
    ts, step = TOKEN_TILE, ROUTE_STEP
    tok = lambda i: (i, 0)
    fixed2 = lambda i: (0, 0)
    return pl.pallas_call(
        functools.partial(_oproj_route_kernel, n_grp=n_grp, n_exp=n_exp),
        grid=(t // step,),
        in_specs=[pl.BlockSpec((step, d), tok),
                  pl.BlockSpec((d, d), fixed2),
                  pl.BlockSpec((step, d), tok),
                  pl.BlockSpec((1, d), fixed2),
                  pl.BlockSpec((d, LANES), fixed2),
                  pl.BlockSpec((1, LANES), fixed2)],
        out_specs=(pl.BlockSpec((step, d), tok),
                   pl.BlockSpec((step, d // 2 + LANES), tok),
                   pl.BlockSpec((step // ts, SUBLANES, ts), lambda i: (i, 0, 0))),
        out_shape=_route_out(t, d, ts),
        compiler_params=_params("arbitrary"),
        name="oproj_route",
    )(o, wo, h, gffn, wr, br)


def _router_weights(w_grp, b_grp, w_rt, b_rt):
    d, n_grp = w_grp.shape
    n_exp = w_rt.shape[2]
    w = jnp.concatenate([w_grp, jnp.transpose(w_rt, (1, 0, 2)).reshape(d, n_grp * n_exp)], axis=1)
    b = jnp.concatenate([b_grp, b_rt.reshape(-1)])
    pad = LANES - w.shape[1]
    return (jnp.pad(w, ((0, 0), (0, pad))).astype(BF16), jnp.pad(b, (0, pad)).reshape(1, LANES))


def _pair_table(n_grp, n_exp):
    lo, hi = [], []
    for g in range(n_grp):
        for a in range(n_exp):
            for b in range(a + 1, n_exp):
                lo.append(g * n_exp + a)
                hi.append(g * n_exp + b)
    return jnp.array(lo, jnp.int32), jnp.array(hi, jnp.int32)


def _moe_layer(h, xr, cls3, wg, wu, wd, n_grp, n_exp):
    t, d = h.shape
    tm = MOE_TILE
    cls_lo, cls_hi = _pair_table(n_grp, n_exp)
    n_cls = cls_lo.shape[0]
    n_tiles = t // tm + n_cls
    assert n_cls <= CLASS_ROWS and n_tiles < 256
    pos3, tiles = _positions(cls3, tm)
    tiles_c = tiles[:n_cls, 0].astype(jnp.int32)
    ends = jnp.cumsum(tiles_c)
    n_act = ends[-1]
    tail = jnp.where(tiles_c > 0, (ends - 1) * tm, -1).astype(jnp.int32)
    tile_ids = jnp.minimum(jnp.arange(n_tiles, dtype=jnp.int32), n_act - 1)
    tile_cls = jnp.sum(tile_ids[:, None] >= ends[None, :], axis=1)
    xs = _dispatch(jnp.concatenate([tail, n_act.reshape(1)]), pos3, xr, n_tiles, tm)
    ys = _moe(cls_lo[tile_cls], cls_hi[tile_cls], n_act.reshape(1), xs, wg, wu, wd, tm)
    return _combine(pos3, h, ys)


def kernel(x, norm_mix, norm_ffn, pool_w, pool_scale, kv_norm, w_kv, k_norm, w_q, q_norm, w_o,
           moe_w_grp, moe_b_grp, moe_w_rt, moe_b_rt, moe_w_gate, moe_w_up, moe_w_down):
    batch, seq, d = x.shape
    t = batch * seq
    n_grp, n_exp = moe_w_rt.shape[1], moe_w_rt.shape[3]
    n_heads = d // HEAD_DIM
    assert len(POOL_WINDOWS) == pool_w.shape[1] and seq % ROUTE_STEP == 0 and seq % ATTN_TILE == 0
    assert t % PROJ_TILE == 0 and d % PROJ_COLS == 0 and n_grp + n_grp * n_exp <= LANES
    assert n_heads % ATTN_HEADS == 0 and seq % ATTN_KEYS == 0 and ATTN_KEYS % ATTN_TILE == 0
    row = lambda v: v.reshape(1, -1).astype(F32)

    wr0, br0 = _router_weights(moe_w_grp[0], moe_b_grp[0], moe_w_rt[0], moe_b_rt[0])
    h1, xr, cls3 = _pool_route(x.reshape(t, d), row(norm_mix[0]), pool_w[0].astype(BF16),
                               row(pool_scale[0]), row(norm_ffn[0]), wr0, br0, batch, n_grp, n_exp)
    h2 = _moe_layer(h1, xr, cls3, moe_w_gate[0].astype(BF16), moe_w_up[0].astype(BF16),
                    moe_w_down[0].astype(BF16), n_grp, n_exp)

    w_all = jnp.concatenate([w_kv, w_q[0]], axis=1).astype(BF16)
    q_gain = q_norm[0] * (HEAD_DIM ** -0.5 * LOG2E)
    gain_all = jnp.concatenate([jnp.tile(k_norm, n_heads), jnp.ones((d,), F32), jnp.tile(q_gain, n_heads)])
    kvq = _kvq(h2, row(kv_norm), row(norm_mix[1]), w_all, row(gain_all))
    o = _attention(kvq, batch, d)

    wr1, br1 = _router_weights(moe_w_grp[1], moe_b_grp[1], moe_w_rt[1], moe_b_rt[1])
    h3, xr, cls3 = _oproj_route(o, w_o[0].astype(BF16), h2, row(norm_ffn[1]), wr1, br1, n_grp, n_exp)
    h4 = _moe_layer(h3, xr, cls3, moe_w_gate[1].astype(BF16), moe_w_up[1].astype(BF16),
                    moe_w_down[1].astype(BF16), n_grp, n_exp)
    return h4.reshape(batch, seq, d)
```

```python
import functools

import jax
import jax.numpy as jnp
from jax import lax
from jax.experimental import pallas as pl
from jax.experimental.pallas import tpu as pltpu

F32 = jnp.float32
BF16 = jnp.bfloat16
WORD = jnp.uint32

EPS = 1e-6
POOL_WINDOWS = (2, 4, 8, 16)
POOL_HALO = 16
HEAD_DIM = 128
LANES = 128
SUBLANES = 8
MXU_COLS = 256
LOG2E = 1.4426950408889634

TOKEN_TILE = 256
ROUTE_STEP = 512
MOE_TILE = 256
PROJ_TILE = 1024
PROJ_COLS = 1024
ATTN_TILE = 256
ATTN_KEYS = 256
ATTN_HEADS = 8
CLASS_ROWS = 32
CAST_BLOCK_ELEMS = 2 * 1024 * 1024
VMEM_LIMIT = 56 * 1024 * 1024


def _params(*sem):
    return pltpu.CompilerParams(dimension_semantics=sem, vmem_limit_bytes=VMEM_LIMIT)


def _rms_scale(x):
    return lax.rsqrt(jnp.mean(x * x, axis=-1, keepdims=True) + EPS)


def _pack_pairs(first, second):
    return pltpu.pack_elementwise([first, second], packed_dtype=BF16)


def _unpack_pairs(words):
    return tuple(pltpu.unpack_elementwise(words, index=i, packed_dtype=BF16, unpacked_dtype=F32)
                 for i in range(2))


def _cast_kernel(*refs):
    out_ref = refs[-1]
    c0 = 0
    for ref in refs[:-1]:
        out_ref[:, c0:c0 + ref.shape[1]] = ref[...].astype(BF16)
        c0 += ref.shape[1]


def _cast_bf16(parts):
    rows = parts[0].shape[0]
    cols = sum(p.shape[1] for p in parts)
    rb = rows
    while rb * cols > CAST_BLOCK_ELEMS and rb % 32 == 0:
        rb //= 2
    return pl.pallas_call(
        _cast_kernel,
        grid=(rows // rb,),
        in_specs=[pl.BlockSpec((rb, p.shape[1]), lambda i: (i, 0)) for p in parts],
        out_specs=pl.BlockSpec((rb, cols), lambda i: (i, 0)),
        out_shape=jax.ShapeDtypeStruct((rows, cols), BF16),
        compiler_params=_params("arbitrary"),
        name="cast_bf16",
    )(*parts)


def _first_argmax(vals):
    best, idx = vals[0], jnp.zeros(vals[0].shape, jnp.int32)
    for i in range(1, len(vals)):
        upd = vals[i] > best
        best = jnp.where(upd, vals[i], best)
        idx = jnp.where(upd, i, idx)
    return best, idx


def _ffn_norm_route(h, gffn_ref, wr_ref, br_ref, xr_ref, cls_ref, sub, n_grp, n_exp):
    ts, d = h.shape
    half = d // 2
    rows = pl.ds(sub * ts, ts)
    xn = h * _rms_scale(h) * gffn_ref[...]
    xr_ref[rows, :half] = _pack_pairs(xn[:, :half], xn[:, half:])
    logits = jnp.dot(xn.astype(BF16), wr_ref[...], preferred_element_type=F32) + br_ref[...]
    lt = logits.T
    row = lambda i: lt[i:i + 1, :]
    g_best, g_idx = _first_argmax([row(g) for g in range(n_grp)])
    denom = sum(jnp.exp(row(g) - g_best) for g in range(n_grp))
    g_w = 1.0 / denom
    sel = []
    for e in range(n_exp):
        v = row(n_grp + e)
        for g in range(1, n_grp):
            v = jnp.where(g_idx == g, row(n_grp + g * n_exp + e), v)
        sel.append(v)
    v1, i1 = _first_argmax(sel)
    v2 = jnp.full_like(v1, -jnp.inf)
    i2 = jnp.zeros_like(i1)
    for e in range(n_exp):
        upd = (i1 != e) & (sel[e] > v2)
        v2 = jnp.where(upd, sel[e], v2)
        i2 = jnp.where(upd, e, i2)
    t = jnp.exp(v2 - v1)
    w1 = g_w / (1.0 + t)
    w2 = w1 * t
    first_low = i1 < i2
    lo = jnp.where(first_low, i1, i2)
    hi = jnp.where(first_low, i2, i1)
    w_lo = jnp.where(first_low, w1, w2)
    w_hi = jnp.where(first_low, w2, w1)
    n_pairs = n_exp * (n_exp - 1) // 2
    pair = ((lo * (2 * n_exp - 1 - lo)) >> 1) + (hi - lo - 1)
    cls = (g_idx * n_pairs + pair).astype(F32)
    r = lax.broadcasted_iota(jnp.int32, (LANES, ts), 0)
    packed = jnp.where(r == 0, cls, jnp.where(r == 1, w_lo, jnp.where(r == 2, w_hi, 0.0)))
    cls_ref[sub] = packed[:SUBLANES, :]
    xr_ref[rows, half:] = lax.bitcast_convert_type(packed.T, xr_ref.dtype)


def _route_out(t, d, ts):
    return (jax.ShapeDtypeStruct((t, d), F32),
            jax.ShapeDtypeStruct((t, d // 2 + LANES), WORD),
            jax.ShapeDtypeStruct((t // ts, SUBLANES, ts), F32))


def _pool_route_kernel(x_ref, gmix_ref, pw_ref, pscale_ref, gffn_ref, wr_ref, br_ref,
                       h_ref, xr_ref, cls_ref, halo_ref, *, n_grp, n_exp):
    s = pl.program_id(1)
    d = x_ref.shape[1]
    ts = TOKEN_TILE
    c = d // len(POOL_WINDOWS)

    @pl.when(s == 0)
    def _():
        halo_ref[...] = jnp.zeros_like(halo_ref)

    for sub in range(x_ref.shape[0] // ts):
        rows = pl.ds(sub * ts, ts)
        x = x_ref[rows, :]
        xn = x * _rms_scale(x) * gmix_ref[...]
        ext = jnp.concatenate([halo_ref[...], xn], axis=0)
        halo_ref[...] = xn[ts - POOL_HALO:, :]
        pos = (s * x_ref.shape[0] + sub * ts) + lax.broadcasted_iota(jnp.int32, (ts, LANES), 0)
        for gi, w in enumerate(POOL_WINDOWS):
            acc = ext[:, gi * c:(gi + 1) * c]
            k = 1
            while k < w:
                acc = acc + pltpu.roll(acc, k, 0)
                k *= 2
            inv_cnt = 1.0 / jnp.minimum(pos + 1, w).astype(F32)
            inv_cnt = jnp.concatenate([inv_cnt] * (c // LANES), axis=1)
            pooled = acc[POOL_HALO:, :] * inv_cnt - xn[:, gi * c:(gi + 1) * c]
            y = jnp.dot(pooled.astype(BF16), pw_ref[gi], preferred_element_type=F32)
            h_ref[rows, gi * c:(gi + 1) * c] = (x[:, gi * c:(gi + 1) * c]
                                                + y * pscale_ref[:, gi * c:(gi + 1) * c])
        _ffn_norm_route(h_ref[rows, :], gffn_ref, wr_ref, br_ref, xr_ref, cls_ref, sub, n_grp, n_exp)


def _pool_route(x2, gmix, pw, pscale, gffn, wr, br, batch, n_grp, n_exp):
    t, d = x2.shape
    ts, step = TOKEN_TILE, ROUTE_STEP
    seq_steps = t // batch // step
    tok = lambda b, s: (b * seq_steps + s, 0)
    fixed2 = lambda b, s: (0, 0)
    return pl.pallas_call(
        functools.partial(_pool_route_kernel, n_grp=n_grp, n_exp=n_exp),
        grid=(batch, seq_steps),
        in_specs=[pl.BlockSpec((step, d), tok),
                  pl.BlockSpec((1, d), fixed2),
                  pl.BlockSpec(pw.shape, lambda b, s: (0, 0, 0)),
                  pl.BlockSpec((1, d), fixed2),
                  pl.BlockSpec((1, d), fixed2),
                  pl.BlockSpec((d, LANES), fixed2),
                  pl.BlockSpec((1, LANES), fixed2)],
        out_specs=(pl.BlockSpec((step, d), tok),
                   pl.BlockSpec((step, d // 2 + LANES), tok),
                   pl.BlockSpec((step // ts, SUBLANES, ts), lambda b, s: (b * seq_steps + s, 0, 0))),
        out_shape=_route_out(t, d, ts),
        scratch_shapes=[pltpu.VMEM((POOL_HALO, d), F32)],
        compiler_params=_params("arbitrary", "arbitrary"),
        name="pool_route",
    )(x2, gmix, pw, pscale, gffn, wr, br)


def _positions_kernel(cls_ref, pos_ref, tiles_ref, *, tm):
    n_t, _, ts = cls_ref.shape
    ci = lax.broadcasted_iota(jnp.int32, (CLASS_ROWS, ts), 0).astype(F32)

    def onehot(i):
        return (cls_ref[i, 0:1, :] == ci).astype(F32)

    acc = lax.fori_loop(0, n_t, lambda i, a: a + onehot(i), jnp.zeros((CLASS_ROWS, ts), F32))
    counts = jnp.sum(acc, axis=1, keepdims=True)
    tiles = jnp.floor((counts + (tm - 1)) * (1.0 / tm))
    tiles_b = jnp.broadcast_to(tiles, (CLASS_ROWS, LANES))
    tiles_ref[...] = tiles_b
    rr = lax.broadcasted_iota(jnp.int32, (CLASS_ROWS, CLASS_ROWS), 0)
    cc = lax.broadcasted_iota(jnp.int32, (CLASS_ROWS, CLASS_ROWS), 1)
    before = (cc < rr).astype(BF16)
    start = jnp.dot(before, tiles_b.astype(BF16), preferred_element_type=F32)[:, 0:1] * tm
    jj = lax.broadcasted_iota(jnp.int32, (ts, ts), 0)
    ss = lax.broadcasted_iota(jnp.int32, (ts, ts), 1)
    earlier = (jj < ss).astype(BF16)

    def body(i, run):
        oh = onehot(i)
        rank = jnp.dot(oh.astype(BF16), earlier, preferred_element_type=F32)
        pos_ref[i] = jnp.sum(oh * (run + rank), axis=0, keepdims=True).astype(jnp.int32)
        return run + jnp.sum(oh, axis=1, keepdims=True)

    lax.fori_loop(0, n_t, body, start)


def _positions(cls3, tm):
    n_t, _, ts = cls3.shape
    return pl.pallas_call(
        functools.partial(_positions_kernel, tm=tm),
        out_shape=(jax.ShapeDtypeStruct((n_t, 1, ts), jnp.int32),
                   jax.ShapeDtypeStruct((CLASS_ROWS, LANES), F32)),
        compiler_params=pltpu.CompilerParams(vmem_limit_bytes=VMEM_LIMIT),
        name="positions",
    )(cls3)


def _start_rows(n, make):
    for r in range(n):
        make(r).start()


def _dispatch_kernel(tail_ref, pos_ref, xr_ref, xs_hbm, rows_ref, zero_ref, sems, zsem, *, tm, n_tiles):
    i = pl.program_id(0)
    n_steps = pl.num_programs(0)
    td = xr_ref.shape[0]
    n_cls = tail_ref.shape[0] - 1
    slot = i % 2

    @pl.when(i == 0)
    def _():
        zero_ref[...] = jnp.zeros_like(zero_ref)
        n_act = tail_ref[n_cls]

        def zero_tile(start):
            return pltpu.make_async_copy(zero_ref, xs_hbm.at[pl.ds(pl.multiple_of(start, tm), tm)], zsem)

        todo = [(tail_ref[c] >= 0, jnp.maximum(tail_ref[c], 0)) for c in range(n_cls)]
        todo += [(n_act + c < n_tiles, jnp.minimum(n_act + c, n_tiles - 1) * tm) for c in range(n_cls)]
        for needed, start in todo:
            @pl.when(needed)
            def _():
                zero_tile(start).start()
        for needed, start in todo:
            @pl.when(needed)
            def _():
                zero_tile(start).wait()

    def wait_rows(s):
        pltpu.make_async_copy(rows_ref.at[s], xs_hbm.at[pl.ds(0, td)], sems.at[s]).wait()

    rows_ref[slot] = xr_ref[...]
    _start_rows(td, lambda r: pltpu.make_async_copy(
        rows_ref.at[slot, pl.ds(r, 1)], xs_hbm.at[pl.ds(pos_ref[0, 0, r], 1)], sems.at[slot]))

    @pl.when(i > 0)
    def _():
        wait_rows(1 - slot)

    @pl.when(i == n_steps - 1)
    def _():
        wait_rows(slot)


def _dispatch(tail, pos3, xr, n_tiles, tm):
    t, dw = xr.shape
    td = pos3.shape[2]
    return pl.pallas_call(
        functools.partial(_dispatch_kernel, tm=tm, n_tiles=n_tiles),
        grid_spec=pltpu.PrefetchScalarGridSpec(
            num_scalar_prefetch=1,
            grid=(t // td,),
            in_specs=[pl.BlockSpec((1, 1, td), lambda i, tail: (i, 0, 0), memory_space=pltpu.SMEM),
                      pl.BlockSpec((td, dw), lambda i, tail: (i, 0))],
            out_specs=pl.BlockSpec(memory_space=pl.ANY),
            scratch_shapes=[pltpu.VMEM((2, td, dw), WORD),
                            pltpu.VMEM((tm, dw), WORD),
                            pltpu.SemaphoreType.DMA((2,)),
                            pltpu.SemaphoreType.DMA(())]),
        out_shape=jax.ShapeDtypeStruct((n_tiles * tm, dw), WORD),
        compiler_params=_params("arbitrary"),
        name="dispatch",
    )(tail, pos3, xr)


def _moe_kernel(elo_ref, ehi_ref, nact_ref, xs_ref, wg_lo, wg_hi, wu_lo, wu_hi, wd_lo, wd_hi, ys_ref):
    half = ys_ref.shape[1]
    active = pl.program_id(0) < nact_ref[0]

    @pl.when(active)
    def _():
        x_hi, x_lo = _unpack_pairs(xs_ref[:, :half])
        x = jnp.concatenate([x_hi.astype(BF16), x_lo.astype(BF16)], axis=1)

        def expert(wg, wu, lane):
            g = jnp.dot(x, wg[0], preferred_element_type=F32)
            u = jnp.dot(x, wu[0], preferred_element_type=F32)
            gate = lax.bitcast_convert_type(xs_ref[:, half + lane:half + lane + 1], F32)
            return (g * (1.0 / (1.0 + jnp.exp(-g))) * u * gate).astype(BF16)

        y = (jnp.dot(expert(wg_lo, wu_lo, 1), wd_lo[0], preferred_element_type=F32)
             + jnp.dot(expert(wg_hi, wu_hi, 2), wd_hi[0], preferred_element_type=F32))
        ys_ref[...] = _pack_pairs(y[:, :half], y[:, half:])

    @pl.when(jnp.logical_not(active))
    def _():
        ys_ref[...] = jnp.zeros_like(ys_ref)


def _moe(e_lo, e_hi, n_act, xs, wg, wu, wd, tm):
    n_rows, dw = xs.shape
    half = dw - LANES
    d = 2 * half
    f = wg.shape[2]
    tile = lambda i, lo, hi, na: (jnp.minimum(i, na[0] - 1), 0)
    w_lo = lambda i, lo, hi, na: (lo[i], 0, 0)
    w_hi = lambda i, lo, hi, na: (hi[i], 0, 0)
    return pl.pallas_call(
        _moe_kernel,
        grid_spec=pltpu.PrefetchScalarGridSpec(
            num_scalar_prefetch=3,
            grid=(n_rows // tm,),
            in_specs=[pl.BlockSpec((tm, dw), tile),
                      pl.BlockSpec((1, d, f), w_lo), pl.BlockSpec((1, d, f), w_hi),
                      pl.BlockSpec((1, d, f), w_lo), pl.BlockSpec((1, d, f), w_hi),
                      pl.BlockSpec((1, f, d), w_lo), pl.BlockSpec((1, f, d), w_hi)],
            out_specs=pl.BlockSpec((tm, half), lambda i, lo, hi, na: (i, 0))),
        out_shape=jax.ShapeDtypeStruct((n_rows, half), WORD),
        compiler_params=_params("arbitrary"),
        name="moe",
    )(e_lo, e_hi, n_act, xs, wg, wg, wu, wu, wd, wd)


def _combine_kernel(pos_ref, posn_ref, h_ref, ys_hbm, out_ref, rows_ref, sems):
    i = pl.program_id(0)
    n_steps = pl.num_programs(0)
    tc = h_ref.shape[0]
    slot = i % 2

    def gather(p_ref, s):
        _start_rows(tc, lambda r: pltpu.make_async_copy(
            ys_hbm.at[pl.ds(p_ref[0, 0, r], 1)], rows_ref.at[s, pl.ds(r, 1)], sems.at[s]))

    @pl.when(i == 0)
    def _():
        gather(pos_ref, 0)

    @pl.when(i + 1 < n_steps)
    def _():
        gather(posn_ref, 1 - slot)

    pltpu.make_async_copy(ys_hbm.at[pl.ds(0, tc)], rows_ref.at[slot], sems.at[slot]).wait()
    y_hi, y_lo = _unpack_pairs(rows_ref[slot])
    out_ref[...] = h_ref[...] + jnp.concatenate([y_hi, y_lo], axis=1)


def _combine(pos3, h, ys):
    t, d = h.shape
    n_steps, _, tc = pos3.shape
    return pl.pallas_call(
        _combine_kernel,
        grid=(n_steps,),
        in_specs=[pl.BlockSpec((1, 1, tc), lambda i: (i, 0, 0), memory_space=pltpu.SMEM),
                  pl.BlockSpec((1, 1, tc), lambda i: (jnp.minimum(i + 1, n_steps - 1), 0, 0),
                               memory_space=pltpu.SMEM),
                  pl.BlockSpec((tc, d), lambda i: (i, 0)),
                  pl.BlockSpec(memory_space=pl.ANY)],
        out_specs=pl.BlockSpec((tc, d), lambda i: (i, 0)),
        out_shape=jax.ShapeDtypeStruct((t, d), F32),
        scratch_shapes=[pltpu.VMEM((2, tc, d // 2), WORD), pltpu.SemaphoreType.DMA((2,))],
        compiler_params=_params("arbitrary"),
        name="combine",
    )(pos3, pos3, h, ys)


def _kvq_kernel(h_ref, gkv_ref, gq_ref, w_ref, gain_ref, out_ref, xkv_ref, xq_ref, *, d):
    n = pl.program_id(1)
    bn = out_ref.shape[1]
    k_blocks, kv_blocks = d // bn, 2 * d // bn

    @pl.when(n == 0)
    def _():
        h = h_ref[...]
        xhat = h * _rms_scale(h)
        xkv_ref[...] = (xhat * gkv_ref[...]).astype(BF16)
        xq_ref[...] = (xhat * gq_ref[...]).astype(BF16)

    def project(x_ref, head_norm):
        x = x_ref[...]
        for c0 in range(0, bn, MXU_COLS):
            acc = jnp.dot(x, w_ref[:, c0:c0 + MXU_COLS], preferred_element_type=F32)
            if head_norm:
                for hh in range(MXU_COLS // HEAD_DIM):
                    cols = slice(c0 + hh * HEAD_DIM, c0 + (hh + 1) * HEAD_DIM)
                    blk = acc[:, hh * HEAD_DIM:(hh + 1) * HEAD_DIM]
                    out_ref[:, cols] = (blk * _rms_scale(blk) * gain_ref[:, cols]).astype(BF16)
            else:
                out_ref[:, c0:c0 + MXU_COLS] = acc.astype(BF16)

    @pl.when(n < k_blocks)
    def _():
        project(xkv_ref, True)

    @pl.when((n >= k_blocks) & (n < kv_blocks))
    def _():
        project(xkv_ref, False)

    @pl.when(n >= kv_blocks)
    def _():
        project(xq_ref, True)


def _kvq(h, gkv, gq, w_all, gain_all):
    t, d = h.shape
    n_out = w_all.shape[1]
    tm, bn = PROJ_TILE, PROJ_COLS
    return pl.pallas_call(
        functools.partial(_kvq_kernel, d=d),
        grid=(t // tm, n_out // bn),
        in_specs=[pl.BlockSpec((tm, d), lambda i, n: (i, 0)),
                  pl.BlockSpec((1, d), lambda i, n: (0, 0)),
                  pl.BlockSpec((1, d), lambda i, n: (0, 0)),
                  pl.BlockSpec((d, bn), lambda i, n: (0, n)),
                  pl.BlockSpec((1, bn), lambda i, n: (0, n))],
        out_specs=pl.BlockSpec((tm, bn), lambda i, n: (i, n)),
        out_shape=jax.ShapeDtypeStruct((t, n_out), BF16),
        scratch_shapes=[pltpu.VMEM((tm, d), BF16), pltpu.VMEM((tm, d), BF16)],
        compiler_params=_params("arbitrary", "arbitrary"),
        name="kvq",
    )(h, gkv, gq, w_all, gain_all)


def _attn_kernel(q_ref, k_ref, v_ref, o_ref, acc_ref, carry_ref, logit_ref, mass_ref):
    qi = pl.program_id(2)
    tq = q_ref.shape[0]
    tk = ATTN_KEYS
    n_heads = q_ref.shape[1] // HEAD_DIM
    n_before = (qi * tq) // tk
    jj = lax.broadcasted_iota(jnp.int32, (tk, tk), 0)
    ss = lax.broadcasted_iota(jnp.int32, (tk, tk), 1)
    later = (jj > ss).astype(BF16)
    rows = qi * tq + lax.broadcasted_iota(jnp.int32, (tq, tk), 0)
    cols = n_before * tk + lax.broadcasted_iota(jnp.int32, (tq, tk), 1)
    causal = cols < rows

    def keys(j):
        return pl.ds(pl.multiple_of(j * tk, tk), tk)

    def score(j, diagonal):
        partial, softplus = [], []
        for g in range(n_heads):
            hd = slice(g * HEAD_DIM, (g + 1) * HEAD_DIM)
            z = lax.dot_general(q_ref[:, hd], k_ref[keys(j), hd], (((1,), (1,)), ((), ())),
                                preferred_element_type=F32)
            sp = jnp.maximum(z, 0.0) + jnp.log2(1.0 + jnp.exp2(-jnp.abs(z)))
            if diagonal:
                sp = jnp.where(causal, sp, 0.0)
            mass_ref[g] = jnp.broadcast_to(jnp.sum(sp, axis=1, keepdims=True), (tq, LANES))
            partial.append(z - sp)
            softplus.append(sp.astype(BF16))
        after = jnp.dot(jnp.concatenate(softplus, axis=0), later, preferred_element_type=F32)
        for g in range(n_heads):
            logit = partial[g] - after[g * tq:(g + 1) * tq, :]
            if diagonal:
                logit = jnp.where(causal, logit, -jnp.inf)
            logit_ref[g] = logit

    def accumulate(g, j):
        hd = slice(g * HEAD_DIM, (g + 1) * HEAD_DIM)
        c = carry_ref[g]
        a = jnp.exp2(logit_ref[g] - jnp.concatenate([c] * (tk // LANES), axis=1))
        acc_ref[g] += jnp.dot(a.astype(BF16), v_ref[keys(j), hd], preferred_element_type=F32)
        carry_ref[g] = c + mass_ref[g]

    acc_ref[...] = jnp.zeros_like(acc_ref)
    carry_ref[...] = jnp.zeros_like(carry_ref)
    score(n_before, True)

    def body(i, _):
        j = n_before - 1 - i
        for g in range(n_heads):
            accumulate(g, j + 1)
        score(j, False)
        return 0

    lax.fori_loop(0, n_before, body, 0)
    for g in range(n_heads):
        accumulate(g, 0)
        o_ref[:, g * HEAD_DIM:(g + 1) * HEAD_DIM] = acc_ref[g].astype(BF16)


def _attention(kvq, batch, d):
    t = kvq.shape[0]
    seq = t // batch
    gw = ATTN_HEADS * HEAD_DIM
    n_hg = d // gw
    tq = ATTN_TILE
    nq = seq // tq
    return pl.pallas_call(
        _attn_kernel,
        grid=(batch, n_hg, nq),
        in_specs=[pl.BlockSpec((tq, gw), lambda b, h, i: (b * nq + i, 2 * n_hg + h)),
                  pl.BlockSpec((seq, gw), lambda b, h, i: (b, h)),
                  pl.BlockSpec((seq, gw), lambda b, h, i: (b, n_hg + h))],
        out_specs=pl.BlockSpec((tq, gw), lambda b, h, i: (b * nq + i, h)),
        out_shape=jax.ShapeDtypeStruct((t, d), BF16),
        scratch_shapes=[pltpu.VMEM((ATTN_HEADS, tq, HEAD_DIM), F32),
                        pltpu.VMEM((ATTN_HEADS, tq, LANES), F32),
                        pltpu.VMEM((ATTN_HEADS, tq, ATTN_KEYS), F32),
                        pltpu.VMEM((ATTN_HEADS, tq, LANES), F32)],
        compiler_params=_params("arbitrary", "arbitrary", "arbitrary"),
        name="attention",
    )(kvq, kvq, kvq)


def _oproj_route_kernel(o_ref, wo_ref, h_ref, gffn_ref, wr_ref, br_ref,
                        hout_ref, xr_ref, cls_ref, *, n_grp, n_exp):
    ts = TOKEN_TILE
    for sub in range(o_ref.shape[0] // ts):
        rows = pl.ds(sub * ts, ts)
        h = h_ref[rows, :] + jnp.dot(o_ref[rows, :], wo_ref[...], preferred_element_type=F32)
        hout_ref[rows, :] = h
        _ffn_norm_route(h, gffn_ref, wr_ref, br_ref, xr_ref, cls_ref, sub, n_grp, n_exp)


def _oproj_route(o, wo, h, gffn, wr, br, n_grp, n_exp):
    t, d = h.shape
    ts, step = TOKEN_TILE, ROUTE_STEP
    tok = lambda i: (i, 0)
    fixed2 = lambda i: (0, 0)
    return pl.pallas_call(
        functools.partial(_oproj_route_kernel, n_grp=n_grp, n_exp=n_exp),
        grid=(t // step,),
        in_specs=[pl.BlockSpec((step, d), tok),
                  pl.BlockSpec((d, d), fixed2),
                  pl.BlockSpec((step, d), tok),
                  pl.BlockSpec((1, d), fixed2),
                  pl.BlockSpec((d, LANES), fixed2),
                  pl.BlockSpec((1, LANES), fixed2)],
        out_specs=(pl.BlockSpec((step, d), tok),
                   pl.BlockSpec((step, d // 2 + LANES), tok),
                   pl.BlockSpec((step // ts, SUBLANES, ts), lambda i: (i, 0, 0))),
        out_shape=_route_out(t, d, ts),
        compiler_params=_params("arbitrary"),
        name="oproj_route",
    )(o, wo, h, gffn, wr, br)


def _router_weights(w_grp, b_grp, w_rt, b_rt):
    d, n_grp = w_grp.shape
    n_exp = w_rt.shape[2]
    w = jnp.concatenate([w_grp, jnp.transpose(w_rt, (1, 0, 2)).reshape(d, n_grp * n_exp)], axis=1)
    b = jnp.concatenate([b_grp, b_rt.reshape(-1)])
    pad = LANES - w.shape[1]
    return (jnp.pad(w, ((0, 0), (0, pad))).astype(BF16), jnp.pad(b, (0, pad)).reshape(1, LANES))


def _pair_table(n_grp, n_exp):
    lo, hi = [], []
    for g in range(n_grp):
        for a in range(n_exp):
            for b in range(a + 1, n_exp):
                lo.append(g * n_exp + a)
                hi.append(g * n_exp + b)
    return jnp.array(lo, jnp.int32), jnp.array(hi, jnp.int32)


def _moe_layer(h, xr, cls3, wg, wu, wd, first_expert, n_grp, n_exp):
    t, d = h.shape
    tm = MOE_TILE
    cls_lo, cls_hi = _pair_table(n_grp, n_exp)
    n_cls = cls_lo.shape[0]
    n_tiles = t // tm + n_cls
    assert n_cls <= CLASS_ROWS and n_tiles < 256
    pos3, tiles = _positions(cls3, tm)
    tiles_c = tiles[:n_cls, 0].astype(jnp.int32)
    ends = jnp.cumsum(tiles_c)
    n_act = ends[-1]
    tail = jnp.where(tiles_c > 0, (ends - 1) * tm, -1).astype(jnp.int32)
    tile_ids = jnp.minimum(jnp.arange(n_tiles, dtype=jnp.int32), n_act - 1)
    tile_cls = jnp.sum(tile_ids[:, None] >= ends[None, :], axis=1)
    xs = _dispatch(jnp.concatenate([tail, n_act.reshape(1)]), pos3, xr, n_tiles, tm)
    ys = _moe(first_expert + cls_lo[tile_cls], first_expert + cls_hi[tile_cls], n_act.reshape(1),
              xs, wg, wu, wd, tm)
    return _combine(pos3, h, ys)


def kernel(x, norm_mix, norm_ffn, pool_w, pool_scale, kv_norm, w_kv, k_norm, w_q, q_norm, w_o,
           moe_w_grp, moe_b_grp, moe_w_rt, moe_b_rt, moe_w_gate, moe_w_up, moe_w_down):
    batch, seq, d = x.shape
    t = batch * seq
    n_grp, n_exp = moe_w_rt.shape[1], moe_w_rt.shape[3]
    n_heads = d // HEAD_DIM
    assert len(POOL_WINDOWS) == pool_w.shape[1] and seq % ROUTE_STEP == 0 and seq % ATTN_TILE == 0
    assert t % PROJ_TILE == 0 and d % PROJ_COLS == 0 and n_grp + n_grp * n_exp <= LANES
    assert n_heads % ATTN_HEADS == 0 and seq % ATTN_KEYS == 0 and ATTN_KEYS % ATTN_TILE == 0
    row = lambda v: v.reshape(1, -1).astype(F32)

    n_all = moe_w_gate.shape[1]
    f = moe_w_gate.shape[3]
    wg = _cast_bf16([moe_w_gate.reshape(-1, f)]).reshape(-1, d, f)
    wu = _cast_bf16([moe_w_up.reshape(-1, f)]).reshape(-1, d, f)
    wd = _cast_bf16([moe_w_down.reshape(-1, d)]).reshape(-1, f, d)

    wr0, br0 = _router_weights(moe_w_grp[0], moe_b_grp[0], moe_w_rt[0], moe_b_rt[0])
    h1, xr, cls3 = _pool_route(x.reshape(t, d), row(norm_mix[0]), pool_w[0].astype(BF16),
                               row(pool_scale[0]), row(norm_ffn[0]), wr0, br0, batch, n_grp, n_exp)
    h2 = _moe_layer(h1, xr, cls3, wg, wu, wd, 0, n_grp, n_exp)

    w_all = _cast_bf16([w_kv, w_q[0]])
    q_gain = q_norm[0] * (HEAD_DIM ** -0.5 * LOG2E)
    gain_all = jnp.concatenate([jnp.tile(k_norm, n_heads), jnp.ones((d,), F32), jnp.tile(q_gain, n_heads)])
    kvq = _kvq(h2, row(kv_norm), row(norm_mix[1]), w_all, row(gain_all))
    o = _attention(kvq, batch, d)

    wr1, br1 = _router_weights(moe_w_grp[1], moe_b_grp[1], moe_w_rt[1], moe_b_rt[1])
    h3, xr, cls3 = _oproj_route(o, _cast_bf16([w_o[0]]), h2, row(norm_ffn[1]), wr1, br1, n_grp, n_exp)
    h4 = _moe_layer(h3, xr, cls3, wg, wu, wd, n_all, n_grp, n_exp)
    return h4.reshape(batch, seq, d)
```

```python
import functools

import jax
import jax.numpy as jnp
from jax import lax
from jax.experimental import pallas as pl
from jax.experimental.pallas import tpu as pltpu

F32 = jnp.float32
BF16 = jnp.bfloat16
WORD = jnp.uint32

EPS = 1e-6
POOL_WINDOWS = (2, 4, 8, 16)
POOL_HALO = 16
HEAD_DIM = 128
LANES = 128
SUBLANES = 8
MXU_COLS = 256
LOG2E = 1.4426950408889634

TOKEN_TILE = 256
ROUTE_STEP = 512
MOE_TILE = 256
PROJ_TILE = 1024
PROJ_COLS = 1024
ATTN_TILE = 256
ATTN_KEYS = 256
ATTN_DEAD_MASS = 160.0
ATTN_HEADS = 8
CLASS_ROWS = 32
CAST_BLOCK_ELEMS = 2 * 1024 * 1024
VMEM_LIMIT = 56 * 1024 * 1024


def _params(*sem):
    return pltpu.CompilerParams(dimension_semantics=sem, vmem_limit_bytes=VMEM_LIMIT)


def _rms_scale(x):
    return lax.rsqrt(jnp.mean(x * x, axis=-1, keepdims=True) + EPS)


def _pack_pairs(first, second):
    return pltpu.pack_elementwise([first, second], packed_dtype=BF16)


def _unpack_pairs(words):
    return tuple(pltpu.unpack_elementwise(words, index=i, packed_dtype=BF16, unpacked_dtype=F32)
                 for i in range(2))


def _cast_kernel(*refs):
    out_ref = refs[-1]
    c0 = 0
    for ref in refs[:-1]:
        out_ref[:, c0:c0 + ref.shape[1]] = ref[...].astype(BF16)
        c0 += ref.shape[1]


def _cast_bf16(parts):
    rows = parts[0].shape[0]
    cols = sum(p.shape[1] for p in parts)
    rb = rows
    while rb * cols > CAST_BLOCK_ELEMS and rb % 32 == 0:
        rb //= 2
    return pl.pallas_call(
        _cast_kernel,
        grid=(rows // rb,),
        in_specs=[pl.BlockSpec((rb, p.shape[1]), lambda i: (i, 0)) for p in parts],
        out_specs=pl.BlockSpec((rb, cols), lambda i: (i, 0)),
        out_shape=jax.ShapeDtypeStruct((rows, cols), BF16),
        compiler_params=_params("arbitrary"),
        name="cast_bf16",
    )(*parts)


def _first_argmax(vals):
    best, idx = vals[0], jnp.zeros(vals[0].shape, jnp.int32)
    for i in range(1, len(vals)):
        upd = vals[i] > best
        best = jnp.where(upd, vals[i], best)
        idx = jnp.where(upd, i, idx)
    return best, idx


def _ffn_norm_route(h, gffn_ref, wr_ref, br_ref, xr_ref, cls_ref, sub, n_grp, n_exp):
    ts, d = h.shape
    half = d // 2
    rows = pl.ds(sub * ts, ts)
    xn = h * _rms_scale(h) * gffn_ref[...]
    xr_ref[rows, :half] = _pack_pairs(xn[:, :half], xn[:, half:])
    logits = jnp.dot(xn.astype(BF16), wr_ref[...], preferred_element_type=F32) + br_ref[...]
    lt = logits.T
    row = lambda i: lt[i:i + 1, :]
    g_best, g_idx = _first_argmax([row(g) for g in range(n_grp)])
    denom = sum(jnp.exp(row(g) - g_best) for g in range(n_grp))
    g_w = 1.0 / denom
    sel = []
    for e in range(n_exp):
        v = row(n_grp + e)
        for g in range(1, n_grp):
            v = jnp.where(g_idx == g, row(n_grp + g * n_exp + e), v)
        sel.append(v)
    v1, i1 = _first_argmax(sel)
    v2 = jnp.full_like(v1, -jnp.inf)
    i2 = jnp.zeros_like(i1)
    for e in range(n_exp):
        upd = (i1 != e) & (sel[e] > v2)
        v2 = jnp.where(upd, sel[e], v2)
        i2 = jnp.where(upd, e, i2)
    t = jnp.exp(v2 - v1)
    w1 = g_w / (1.0 + t)
    w2 = w1 * t
    first_low = i1 < i2
    lo = jnp.where(first_low, i1, i2)
    hi = jnp.where(first_low, i2, i1)
    w_lo = jnp.where(first_low, w1, w2)
    w_hi = jnp.where(first_low, w2, w1)
    n_pairs = n_exp * (n_exp - 1) // 2
    pair = ((lo * (2 * n_exp - 1 - lo)) >> 1) + (hi - lo - 1)
    cls = (g_idx * n_pairs + pair).astype(F32)
    r = lax.broadcasted_iota(jnp.int32, (LANES, ts), 0)
    packed = jnp.where(r == 0, cls, jnp.where(r == 1, w_lo, jnp.where(r == 2, w_hi, 0.0)))
    cls_ref[sub] = packed[:SUBLANES, :]
    xr_ref[rows, half:] = lax.bitcast_convert_type(packed.T, xr_ref.dtype)


def _route_out(t, d, ts):
    return (jax.ShapeDtypeStruct((t, d), F32),
            jax.ShapeDtypeStruct((t, d // 2 + LANES), WORD),
            jax.ShapeDtypeStruct((t // ts, SUBLANES, ts), F32))


def _pool_route_kernel(x_ref, gmix_ref, pw_ref, pscale_ref, gffn_ref, wr_ref, br_ref,
                       h_ref, xr_ref, cls_ref, halo_ref, *, n_grp, n_exp):
    s = pl.program_id(1)
    d = x_ref.shape[1]
    ts = TOKEN_TILE
    c = d // len(POOL_WINDOWS)

    @pl.when(s == 0)
    def _():
        halo_ref[...] = jnp.zeros_like(halo_ref)

    for sub in range(x_ref.shape[0] // ts):
        rows = pl.ds(sub * ts, ts)
        x = x_ref[rows, :]
        xn = x * _rms_scale(x) * gmix_ref[...]
        ext = jnp.concatenate([halo_ref[...], xn], axis=0)
        halo_ref[...] = xn[ts - POOL_HALO:, :]
        pos = (s * x_ref.shape[0] + sub * ts) + lax.broadcasted_iota(jnp.int32, (ts, LANES), 0)
        for gi, w in enumerate(POOL_WINDOWS):
            acc = ext[:, gi * c:(gi + 1) * c]
            k = 1
            while k < w:
                acc = acc + pltpu.roll(acc, k, 0)
                k *= 2
            inv_cnt = 1.0 / jnp.minimum(pos + 1, w).astype(F32)
            inv_cnt = jnp.concatenate([inv_cnt] * (c // LANES), axis=1)
            pooled = acc[POOL_HALO:, :] * inv_cnt - xn[:, gi * c:(gi + 1) * c]
            y = jnp.dot(pooled.astype(BF16), pw_ref[gi], preferred_element_type=F32)
            h_ref[rows, gi * c:(gi + 1) * c] = (x[:, gi * c:(gi + 1) * c]
                                                + y * pscale_ref[:, gi * c:(gi + 1) * c])
        _ffn_norm_route(h_ref[rows, :], gffn_ref, wr_ref, br_ref, xr_ref, cls_ref, sub, n_grp, n_exp)


def _pool_route(x2, gmix, pw, pscale, gffn, wr, br, batch, n_grp, n_exp):
    t, d = x2.shape
    ts, step = TOKEN_TILE, ROUTE_STEP
    seq_steps = t // batch // step
    tok = lambda b, s: (b * seq_steps + s, 0)
    fixed2 = lambda b, s: (0, 0)
    return pl.pallas_call(
        functools.partial(_pool_route_kernel, n_grp=n_grp, n_exp=n_exp),
        grid=(batch, seq_steps),
        in_specs=[pl.BlockSpec((step, d), tok),
                  pl.BlockSpec((1, d), fixed2),
                  pl.BlockSpec(pw.shape, lambda b, s: (0, 0, 0)),
                  pl.BlockSpec((1, d), fixed2),
                  pl.BlockSpec((1, d), fixed2),
                  pl.BlockSpec((d, LANES), fixed2),
                  pl.BlockSpec((1, LANES), fixed2)],
        out_specs=(pl.BlockSpec((step, d), tok),
                   pl.BlockSpec((step, d // 2 + LANES), tok),
                   pl.BlockSpec((step // ts, SUBLANES, ts), lambda b, s: (b * seq_steps + s, 0, 0))),
        out_shape=_route_out(t, d, ts),
        scratch_shapes=[pltpu.VMEM((POOL_HALO, d), F32)],
        compiler_params=_params("arbitrary", "arbitrary"),
        name="pool_route",
    )(x2, gmix, pw, pscale, gffn, wr, br)


def _positions_kernel(cls_ref, pos_ref, tiles_ref, *, tm):
    n_t, _, ts = cls_ref.shape
    ci = lax.broadcasted_iota(jnp.int32, (CLASS_ROWS, ts), 0).astype(F32)

    def onehot(i):
        return (cls_ref[i, 0:1, :] == ci).astype(F32)

    acc = lax.fori_loop(0, n_t, lambda i, a: a + onehot(i), jnp.zeros((CLASS_ROWS, ts), F32))
    counts = jnp.sum(acc, axis=1, keepdims=True)
    tiles = jnp.floor((counts + (tm - 1)) * (1.0 / tm))
    tiles_b = jnp.broadcast_to(tiles, (CLASS_ROWS, LANES))
    tiles_ref[...] = tiles_b
    rr = lax.broadcasted_iota(jnp.int32, (CLASS_ROWS, CLASS_ROWS), 0)
    cc = lax.broadcasted_iota(jnp.int32, (CLASS_ROWS, CLASS_ROWS), 1)
    before = (cc < rr).astype(BF16)
    start = jnp.dot(before, tiles_b.astype(BF16), preferred_element_type=F32)[:, 0:1] * tm
    jj = lax.broadcasted_iota(jnp.int32, (ts, ts), 0)
    ss = lax.broadcasted_iota(jnp.int32, (ts, ts), 1)
    earlier = (jj < ss).astype(BF16)

    def body(i, run):
        oh = onehot(i)
        rank = jnp.dot(oh.astype(BF16), earlier, preferred_element_type=F32)
        pos_ref[i] = jnp.sum(oh * (run + rank), axis=0, keepdims=True).astype(jnp.int32)
        return run + jnp.sum(oh, axis=1, keepdims=True)

    lax.fori_loop(0, n_t, body, start)


def _positions(cls3, tm):
    n_t, _, ts = cls3.shape
    return pl.pallas_call(
        functools.partial(_positions_kernel, tm=tm),
        out_shape=(jax.ShapeDtypeStruct((n_t, 1, ts), jnp.int32),
                   jax.ShapeDtypeStruct((CLASS_ROWS, LANES), F32)),
        compiler_params=pltpu.CompilerParams(vmem_limit_bytes=VMEM_LIMIT),
        name="positions",
    )(cls3)


def _start_rows(n, make):
    for r in range(n):
        make(r).start()


def _dispatch_kernel(tail_ref, pos_ref, xr_ref, xs_hbm, rows_ref, zero_ref, sems, zsem, *, tm, n_tiles):
    i = pl.program_id(0)
    n_steps = pl.num_programs(0)
    td = xr_ref.shape[0]
    n_cls = tail_ref.shape[0] - 1
    slot = i % 2

    @pl.when(i == 0)
    def _():
        zero_ref[...] = jnp.zeros_like(zero_ref)
        n_act = tail_ref[n_cls]

        def zero_tile(start):
            return pltpu.make_async_copy(zero_ref, xs_hbm.at[pl.ds(pl.multiple_of(start, tm), tm)], zsem)

        todo = [(tail_ref[c] >= 0, jnp.maximum(tail_ref[c], 0)) for c in range(n_cls)]
        todo += [(n_act + c < n_tiles, jnp.minimum(n_act + c, n_tiles - 1) * tm) for c in range(n_cls)]
        for needed, start in todo:
            @pl.when(needed)
            def _():
                zero_tile(start).start()
        for needed, start in todo:
            @pl.when(needed)
            def _():
                zero_tile(start).wait()

    def wait_rows(s):
        pltpu.make_async_copy(rows_ref.at[s], xs_hbm.at[pl.ds(0, td)], sems.at[s]).wait()

    rows_ref[slot] = xr_ref[...]
    _start_rows(td, lambda r: pltpu.make_async_copy(
        rows_ref.at[slot, pl.ds(r, 1)], xs_hbm.at[pl.ds(pos_ref[0, 0, r], 1)], sems.at[slot]))

    @pl.when(i > 0)
    def _():
        wait_rows(1 - slot)

    @pl.when(i == n_steps - 1)
    def _():
        wait_rows(slot)


def _dispatch(tail, pos3, xr, n_tiles, tm):
    t, dw = xr.shape
    td = pos3.shape[2]
    return pl.pallas_call(
        functools.partial(_dispatch_kernel, tm=tm, n_tiles=n_tiles),
        grid_spec=pltpu.PrefetchScalarGridSpec(
            num_scalar_prefetch=1,
            grid=(t // td,),
            in_specs=[pl.BlockSpec((1, 1, td), lambda i, tail: (i, 0, 0), memory_space=pltpu.SMEM),
                      pl.BlockSpec((td, dw), lambda i, tail: (i, 0))],
            out_specs=pl.BlockSpec(memory_space=pl.ANY),
            scratch_shapes=[pltpu.VMEM((2, td, dw), WORD),
                            pltpu.VMEM((tm, dw), WORD),
                            pltpu.SemaphoreType.DMA((2,)),
                            pltpu.SemaphoreType.DMA(())]),
        out_shape=jax.ShapeDtypeStruct((n_tiles * tm, dw), WORD),
        compiler_params=_params("arbitrary"),
        name="dispatch",
    )(tail, pos3, xr)


def _moe_kernel(elo_ref, ehi_ref, nact_ref, xs_ref, wg_lo, wg_hi, wu_lo, wu_hi, wd_lo, wd_hi, ys_ref):
    half = ys_ref.shape[1]
    active = pl.program_id(0) < nact_ref[0]

    @pl.when(active)
    def _():
        x_hi, x_lo = _unpack_pairs(xs_ref[:, :half])
        x = jnp.concatenate([x_hi.astype(BF16), x_lo.astype(BF16)], axis=1)

        def expert(wg, wu, lane):
            g = jnp.dot(x, wg[0], preferred_element_type=F32)
            u = jnp.dot(x, wu[0], preferred_element_type=F32)
            gate = lax.bitcast_convert_type(xs_ref[:, half + lane:half + lane + 1], F32)
            return (g * (1.0 / (1.0 + jnp.exp(-g))) * u * gate).astype(BF16)

        y = (jnp.dot(expert(wg_lo, wu_lo, 1), wd_lo[0], preferred_element_type=F32)
             + jnp.dot(expert(wg_hi, wu_hi, 2), wd_hi[0], preferred_element_type=F32))
        ys_ref[...] = _pack_pairs(y[:, :half], y[:, half:])

    @pl.when(jnp.logical_not(active))
    def _():
        ys_ref[...] = jnp.zeros_like(ys_ref)


def _moe(e_lo, e_hi, n_act, xs, wg, wu, wd, tm):
    n_rows, dw = xs.shape
    half = dw - LANES
    d = 2 * half
    f = wg.shape[2]
    tile = lambda i, lo, hi, na: (jnp.minimum(i, na[0] - 1), 0)
    w_lo = lambda i, lo, hi, na: (lo[i], 0, 0)
    w_hi = lambda i, lo, hi, na: (hi[i], 0, 0)
    return pl.pallas_call(
        _moe_kernel,
        grid_spec=pltpu.PrefetchScalarGridSpec(
            num_scalar_prefetch=3,
            grid=(n_rows // tm,),
            in_specs=[pl.BlockSpec((tm, dw), tile),
                      pl.BlockSpec((1, d, f), w_lo), pl.BlockSpec((1, d, f), w_hi),
                      pl.BlockSpec((1, d, f), w_lo), pl.BlockSpec((1, d, f), w_hi),
                      pl.BlockSpec((1, f, d), w_lo), pl.BlockSpec((1, f, d), w_hi)],
            out_specs=pl.BlockSpec((tm, half), lambda i, lo, hi, na: (i, 0))),
        out_shape=jax.ShapeDtypeStruct((n_rows, half), WORD),
        compiler_params=_params("arbitrary"),
        name="moe",
    )(e_lo, e_hi, n_act, xs, wg, wg, wu, wu, wd, wd)


def _combine_kernel(pos_ref, posn_ref, h_ref, ys_hbm, out_ref, rows_ref, sems):
    i = pl.program_id(0)
    n_steps = pl.num_programs(0)
    tc = h_ref.shape[0]
    slot = i % 2

    def gather(p_ref, s):
        _start_rows(tc, lambda r: pltpu.make_async_copy(
            ys_hbm.at[pl.ds(p_ref[0, 0, r], 1)], rows_ref.at[s, pl.ds(r, 1)], sems.at[s]))

    @pl.when(i == 0)
    def _():
        gather(pos_ref, 0)

    @pl.when(i + 1 < n_steps)
    def _():
        gather(posn_ref, 1 - slot)

    pltpu.make_async_copy(ys_hbm.at[pl.ds(0, tc)], rows_ref.at[slot], sems.at[slot]).wait()
    y_hi, y_lo = _unpack_pairs(rows_ref[slot])
    out_ref[...] = h_ref[...] + jnp.concatenate([y_hi, y_lo], axis=1)


def _combine(pos3, h, ys):
    t, d = h.shape
    n_steps, _, tc = pos3.shape
    return pl.pallas_call(
        _combine_kernel,
        grid=(n_steps,),
        in_specs=[pl.BlockSpec((1, 1, tc), lambda i: (i, 0, 0), memory_space=pltpu.SMEM),
                  pl.BlockSpec((1, 1, tc), lambda i: (jnp.minimum(i + 1, n_steps - 1), 0, 0),
                               memory_space=pltpu.SMEM),
                  pl.BlockSpec((tc, d), lambda i: (i, 0)),
                  pl.BlockSpec(memory_space=pl.ANY)],
        out_specs=pl.BlockSpec((tc, d), lambda i: (i, 0)),
        out_shape=jax.ShapeDtypeStruct((t, d), F32),
        scratch_shapes=[pltpu.VMEM((2, tc, d // 2), WORD), pltpu.SemaphoreType.DMA((2,))],
        compiler_params=_params("arbitrary"),
        name="combine",
    )(pos3, pos3, h, ys)


def _kvq_kernel(h_ref, gkv_ref, gq_ref, w_ref, gain_ref, out_ref, xkv_ref, xq_ref, *, d):
    n = pl.program_id(1)
    bn = out_ref.shape[1]
    k_blocks, kv_blocks = d // bn, 2 * d // bn

    @pl.when(n == 0)
    def _():
        h = h_ref[...]
        xhat = h * _rms_scale(h)
        xkv_ref[...] = (xhat * gkv_ref[...]).astype(BF16)
        xq_ref[...] = (xhat * gq_ref[...]).astype(BF16)

    def project(x_ref, head_norm):
        x = x_ref[...]
        for c0 in range(0, bn, MXU_COLS):
            acc = jnp.dot(x, w_ref[:, c0:c0 + MXU_COLS], preferred_element_type=F32)
            if head_norm:
                for hh in range(MXU_COLS // HEAD_DIM):
                    cols = slice(c0 + hh * HEAD_DIM, c0 + (hh + 1) * HEAD_DIM)
                    blk = acc[:, hh * HEAD_DIM:(hh + 1) * HEAD_DIM]
                    out_ref[:, cols] = (blk * _rms_scale(blk) * gain_ref[:, cols]).astype(BF16)
            else:
                out_ref[:, c0:c0 + MXU_COLS] = acc.astype(BF16)

    @pl.when(n < k_blocks)
    def _():
        project(xkv_ref, True)

    @pl.when((n >= k_blocks) & (n < kv_blocks))
    def _():
        project(xkv_ref, False)

    @pl.when(n >= kv_blocks)
    def _():
        project(xq_ref, True)


def _kvq(h, gkv, gq, w_all, gain_all):
    t, d = h.shape
    n_out = w_all.shape[1]
    tm, bn = PROJ_TILE, PROJ_COLS
    return pl.pallas_call(
        functools.partial(_kvq_kernel, d=d),
        grid=(t // tm, n_out // bn),
        in_specs=[pl.BlockSpec((tm, d), lambda i, n: (i, 0)),
                  pl.BlockSpec((1, d), lambda i, n: (0, 0)),
                  pl.BlockSpec((1, d), lambda i, n: (0, 0)),
                  pl.BlockSpec((d, bn), lambda i, n: (0, n)),
                  pl.BlockSpec((1, bn), lambda i, n: (0, n))],
        out_specs=pl.BlockSpec((tm, bn), lambda i, n: (i, n)),
        out_shape=jax.ShapeDtypeStruct((t, n_out), BF16),
        scratch_shapes=[pltpu.VMEM((tm, d), BF16), pltpu.VMEM((tm, d), BF16)],
        compiler_params=_params("arbitrary", "arbitrary"),
        name="kvq",
    )(h, gkv, gq, w_all, gain_all)


def _attn_kernel(q_ref, k_ref, v_ref, o_ref, acc_ref, carry_ref, logit_ref, mass_ref):
    qi = pl.program_id(2)
    tq = q_ref.shape[0]
    tk = ATTN_KEYS
    n_heads = q_ref.shape[1] // HEAD_DIM
    n_before = (qi * tq) // tk
    jj = lax.broadcasted_iota(jnp.int32, (tk, tk), 0)
    ss = lax.broadcasted_iota(jnp.int32, (tk, tk), 1)
    later = (jj > ss).astype(BF16)
    rows = qi * tq + lax.broadcasted_iota(jnp.int32, (tq, tk), 0)
    cols = n_before * tk + lax.broadcasted_iota(jnp.int32, (tq, tk), 1)
    causal = cols < rows
    sign_bit = jnp.uint32(0x80000000)

    def keys(j):
        return pl.ds(pl.multiple_of(j * tk, tk), tk)

    def score(j, diagonal):
        partial, softplus, first = [], [], []
        for g in range(n_heads):
            hd = slice(g * HEAD_DIM, (g + 1) * HEAD_DIM)
            z = lax.dot_general(q_ref[:, hd], k_ref[keys(j), hd], (((1,), (1,)), ((), ())),
                                preferred_element_type=F32)
            neg_abs = lax.bitcast_convert_type(lax.bitcast_convert_type(z, WORD) | sign_bit, F32)
            sp = jnp.maximum(z, 0.0) + jnp.log2(1.0 + jnp.exp2(neg_abs))
            if diagonal:
                sp = jnp.where(causal, sp, 0.0)
            partial.append(z - sp)
            softplus.append(sp.astype(BF16))
            first.append(sp[:, 0:1])
        after = jnp.dot(jnp.concatenate(softplus, axis=0), later, preferred_element_type=F32)
        for g in range(n_heads):
            after_g = after[g * tq:(g + 1) * tq, :]
            logit = partial[g] - after_g
            if diagonal:
                logit = jnp.where(causal, logit, -jnp.inf)
            logit_ref[g] = logit
            mass_ref[g] = jnp.broadcast_to(first[g] + after_g[:, 0:1], (tq, LANES))

    def accumulate(g, j):
        hd = slice(g * HEAD_DIM, (g + 1) * HEAD_DIM)
        c = carry_ref[g]
        a = jnp.exp2(logit_ref[g] - jnp.concatenate([c] * (tk // LANES), axis=1))
        acc_ref[g] += jnp.dot(a.astype(BF16), v_ref[keys(j), hd], preferred_element_type=F32)
        carry_ref[g] = c + mass_ref[g]

    def alive():
        return jnp.min(carry_ref[...] + mass_ref[...]) < ATTN_DEAD_MASS

    acc_ref[...] = jnp.zeros_like(acc_ref)
    carry_ref[...] = jnp.zeros_like(carry_ref)
    score(n_before, True)

    def body(state):
        i, _ = state
        j = n_before - 1 - i
        for g in range(n_heads):
            accumulate(g, j + 1)
        score(j, False)
        return i + 1, alive()

    n_done, _ = lax.while_loop(lambda state: (state[0] < n_before) & state[1], body,
                               (jnp.int32(0), alive()))
    for g in range(n_heads):
        accumulate(g, n_before - n_done)
        o_ref[:, g * HEAD_DIM:(g + 1) * HEAD_DIM] = acc_ref[g].astype(BF16)


def _attention(kvq, batch, d):
    t = kvq.shape[0]
    seq = t // batch
    gw = ATTN_HEADS * HEAD_DIM
    n_hg = d // gw
    tq = ATTN_TILE
    nq = seq // tq
    return pl.pallas_call(
        _attn_kernel,
        grid=(batch, n_hg, nq),
        in_specs=[pl.BlockSpec((tq, gw), lambda b, h, i: (b * nq + i, 2 * n_hg + h)),
                  pl.BlockSpec((seq, gw), lambda b, h, i: (b, h)),
                  pl.BlockSpec((seq, gw), lambda b, h, i: (b, n_hg + h))],
        out_specs=pl.BlockSpec((tq, gw), lambda b, h, i: (b * nq + i, h)),
        out_shape=jax.ShapeDtypeStruct((t, d), BF16),
        scratch_shapes=[pltpu.VMEM((ATTN_HEADS, tq, HEAD_DIM), F32),
                        pltpu.VMEM((ATTN_HEADS, tq, LANES), F32),
                        pltpu.VMEM((ATTN_HEADS, tq, ATTN_KEYS), F32),
                        pltpu.VMEM((ATTN_HEADS, tq, LANES), F32)],
        compiler_params=_params("arbitrary", "arbitrary", "arbitrary"),
        name="attention",
    )(kvq, kvq, kvq)


def _oproj_route_kernel(o_ref, wo_ref, h_ref, gffn_ref, wr_ref, br_ref,
                        hout_ref, xr_ref, cls_ref, *, n_grp, n_exp):
    ts = TOKEN_TILE
    for sub in range(o_ref.shape[0] // ts):
        rows = pl.ds(sub * ts, ts)
        h = h_ref[rows, :] + jnp.dot(o_ref[rows, :], wo_ref[...], preferred_element_type=F32)
        hout_ref[rows, :] = h
        _ffn_norm_route(h, gffn_ref, wr_ref, br_ref, xr_ref, cls_ref, sub, n_grp, n_exp)


def _oproj_route(o, wo, h, gffn, wr, br, n_grp, n_exp):
    t, d = h.shape
    ts, step = TOKEN_TILE, ROUTE_STEP
    tok = lambda i: (i, 0)
    fixed2 = lambda i: (0, 0)
    return pl.pallas_call(
        functools.partial(_oproj_route_kernel, n_grp=n_grp, n_exp=n_exp),
        grid=(t // step,),
        in_specs=[pl.BlockSpec((step, d), tok),
                  pl.BlockSpec((d, d), fixed2),
                  pl.BlockSpec((step, d), tok),
                  pl.BlockSpec((1, d), fixed2),
                  pl.BlockSpec((d, LANES), fixed2),
                  pl.BlockSpec((1, LANES), fixed2)],
        out_specs=(pl.BlockSpec((step, d), tok),
                   pl.BlockSpec((step, d // 2 + LANES), tok),
                   pl.BlockSpec((step // ts, SUBLANES, ts), lambda i: (i, 0, 0))),
        out_shape=_route_out(t, d, ts),
        compiler_params=_params("arbitrary"),
        name="oproj_route",
    )(o, wo, h, gffn, wr, br)


def _router_weights(w_grp, b_grp, w_rt, b_rt):
    d, n_grp = w_grp.shape
    n_exp = w_rt.shape[2]
    w = jnp.concatenate([w_grp, jnp.transpose(w_rt, (1, 0, 2)).reshape(d, n_grp * n_exp)], axis=1)
    b = jnp.concatenate([b_grp, b_rt.reshape(-1)])
    pad = LANES - w.shape[1]
    return (jnp.pad(w, ((0, 0), (0, pad))).astype(BF16), jnp.pad(b, (0, pad)).reshape(1, LANES))


def _pair_table(n_grp, n_exp):
    lo, hi = [], []
    for g in range(n_grp):
        for a in range(n_exp):
            for b in range(a + 1, n_exp):
                lo.append(g * n_exp + a)
                hi.append(g * n_exp + b)
    return jnp.array(lo, jnp.int32), jnp.array(hi, jnp.int32)


def _moe_layer(h, xr, cls3, wg, wu, wd, first_expert, n_grp, n_exp):
    t, d = h.shape
    tm = MOE_TILE
    cls_lo, cls_hi = _pair_table(n_grp, n_exp)
    n_cls = cls_lo.shape[0]
    n_tiles = t // tm + n_cls
    assert n_cls <= CLASS_ROWS and n_tiles < 256
    pos3, tiles = _positions(cls3, tm)
    tiles_c = tiles[:n_cls, 0].astype(jnp.int32)
    ends = jnp.cumsum(tiles_c)
    n_act = ends[-1]
    tail = jnp.where(tiles_c > 0, (ends - 1) * tm, -1).astype(jnp.int32)
    tile_ids = jnp.minimum(jnp.arange(n_tiles, dtype=jnp.int32), n_act - 1)
    tile_cls = jnp.sum(tile_ids[:, None] >= ends[None, :], axis=1)
    xs = _dispatch(jnp.concatenate([tail, n_act.reshape(1)]), pos3, xr, n_tiles, tm)
    ys = _moe(first_expert + cls_lo[tile_cls], first_expert + cls_hi[tile_cls], n_act.reshape(1),
              xs, wg, wu, wd, tm)
    return _combine(pos3, h, ys)


def kernel(x, norm_mix, norm_ffn, pool_w, pool_scale, kv_norm, w_kv, k_norm, w_q, q_norm, w_o,
           moe_w_grp, moe_b_grp, moe_w_rt, moe_b_rt, moe_w_gate, moe_w_up, moe_w_down):
    batch, seq, d = x.shape
    t = batch * seq
    n_grp, n_exp = moe_w_rt.shape[1], moe_w_rt.shape[3]
    n_heads = d // HEAD_DIM
    assert len(POOL_WINDOWS) == pool_w.shape[1] and seq % ROUTE_STEP == 0 and seq % ATTN_TILE == 0
    assert t % PROJ_TILE == 0 and d % PROJ_COLS == 0 and n_grp + n_grp * n_exp <= LANES
    assert n_heads % ATTN_HEADS == 0 and seq % ATTN_KEYS == 0 and ATTN_KEYS % ATTN_TILE == 0
    row = lambda v: v.reshape(1, -1).astype(F32)

    n_all = moe_w_gate.shape[1]
    f = moe_w_gate.shape[3]
    wg = _cast_bf16([moe_w_gate.reshape(-1, f)]).reshape(-1, d, f)
    wu = _cast_bf16([moe_w_up.reshape(-1, f)]).reshape(-1, d, f)
    wd = _cast_bf16([moe_w_down.reshape(-1, d)]).reshape(-1, f, d)

    wr0, br0 = _router_weights(moe_w_grp[0], moe_b_grp[0], moe_w_rt[0], moe_b_rt[0])
    h1, xr, cls3 = _pool_route(x.reshape(t, d), row(norm_mix[0]), pool_w[0].astype(BF16),
                               row(pool_scale[0]), row(norm_ffn[0]), wr0, br0, batch, n_grp, n_exp)
    h2 = _moe_layer(h1, xr, cls3, wg, wu, wd, 0, n_grp, n_exp)

    w_all = _cast_bf16([w_kv, w_q[0]])
    q_gain = q_norm[0] * (HEAD_DIM ** -0.5 * LOG2E)
    gain_all = jnp.concatenate([jnp.tile(k_norm, n_heads), jnp.ones((d,), F32), jnp.tile(q_gain, n_heads)])
    kvq = _kvq(h2, row(kv_norm), row(norm_mix[1]), w_all, row(gain_all))
    o = _attention(kvq, batch, d)

    wr1, br1 = _router_weights(moe_w_grp[1], moe_b_grp[1], moe_w_rt[1], moe_b_rt[1])
    h3, xr, cls3 = _oproj_route(o, _cast_bf16([w_o[0]]), h2, row(norm_ffn[1]), wr1, br1, n_grp, n_exp)
    h4 = _moe_layer(h3, xr, cls3, wg, wu, wd, n_all, n_grp, n_exp)
    return h4.reshape(batch, seq, d)
```

```python
import functools

import jax
import jax.numpy as jnp
from jax import lax
from jax.experimental import pallas as pl
from jax.experimental.pallas import tpu as pltpu

F32 = jnp.float32
BF16 = jnp.bfloat16
WORD = jnp.uint32

EPS = 1e-6
POOL_WINDOWS = (2, 4, 8, 16)
POOL_HALO = 16
HEAD_DIM = 128
LANES = 128
SUBLANES = 8
MXU_COLS = 256
LOG2E = 1.4426950408889634

TOKEN_TILE = 256
ROUTE_STEP = 512
MOE_TILE = 256
PROJ_TILE = 1024
PROJ_COLS = 1024
ATTN_TILE = 256
ATTN_KEYS = 256
ATTN_DEAD_MASS = 160.0
ATTN_HEADS = 8
PERMUTE_SUBTILES = 2
CLASS_ROWS = 32
CAST_BLOCK_ELEMS = 2 * 1024 * 1024
VMEM_LIMIT = 56 * 1024 * 1024


def _params(*sem):
    return pltpu.CompilerParams(dimension_semantics=sem, vmem_limit_bytes=VMEM_LIMIT)


def _rms_scale(x):
    return lax.rsqrt(jnp.mean(x * x, axis=-1, keepdims=True) + EPS)


def _pack_pairs(first, second):
    return pltpu.pack_elementwise([first, second], packed_dtype=BF16)


def _unpack_pairs(words):
    return tuple(pltpu.unpack_elementwise(words, index=i, packed_dtype=BF16, unpacked_dtype=F32)
                 for i in range(2))


def _cast_kernel(*refs):
    out_ref = refs[-1]
    c0 = 0
    for ref in refs[:-1]:
        out_ref[:, c0:c0 + ref.shape[1]] = ref[...].astype(BF16)
        c0 += ref.shape[1]


def _cast_bf16(parts):
    rows = parts[0].shape[0]
    cols = sum(p.shape[1] for p in parts)
    rb = rows
    while rb * cols > CAST_BLOCK_ELEMS and rb % 32 == 0:
        rb //= 2
    return pl.pallas_call(
        _cast_kernel,
        grid=(rows // rb,),
        in_specs=[pl.BlockSpec((rb, p.shape[1]), lambda i: (i, 0)) for p in parts],
        out_specs=pl.BlockSpec((rb, cols), lambda i: (i, 0)),
        out_shape=jax.ShapeDtypeStruct((rows, cols), BF16),
        compiler_params=_params("arbitrary"),
        name="cast_bf16",
    )(*parts)


def _first_argmax(vals):
    best, idx = vals[0], jnp.zeros(vals[0].shape, jnp.int32)
    for i in range(1, len(vals)):
        upd = vals[i] > best
        best = jnp.where(upd, vals[i], best)
        idx = jnp.where(upd, i, idx)
    return best, idx


def _ffn_norm_route(h, gffn_ref, wr_ref, br_ref, xr_ref, cls_ref, sub, n_grp, n_exp):
    ts, d = h.shape
    half = d // 2
    rows = pl.ds(sub * ts, ts)
    xn = h * _rms_scale(h) * gffn_ref[...]
    xr_ref[rows, :half] = _pack_pairs(xn[:, :half], xn[:, half:])
    logits = jnp.dot(xn.astype(BF16), wr_ref[...], preferred_element_type=F32) + br_ref[...]
    lt = logits.T
    row = lambda i: lt[i:i + 1, :]
    g_best, g_idx = _first_argmax([row(g) for g in range(n_grp)])
    denom = sum(jnp.exp(row(g) - g_best) for g in range(n_grp))
    g_w = 1.0 / denom
    sel = []
    for e in range(n_exp):
        v = row(n_grp + e)
        for g in range(1, n_grp):
            v = jnp.where(g_idx == g, row(n_grp + g * n_exp + e), v)
        sel.append(v)
    v1, i1 = _first_argmax(sel)
    v2 = jnp.full_like(v1, -jnp.inf)
    i2 = jnp.zeros_like(i1)
    for e in range(n_exp):
        upd = (i1 != e) & (sel[e] > v2)
        v2 = jnp.where(upd, sel[e], v2)
        i2 = jnp.where(upd, e, i2)
    t = jnp.exp(v2 - v1)
    w1 = g_w / (1.0 + t)
    w2 = w1 * t
    first_low = i1 < i2
    lo = jnp.where(first_low, i1, i2)
    hi = jnp.where(first_low, i2, i1)
    w_lo = jnp.where(first_low, w1, w2)
    w_hi = jnp.where(first_low, w2, w1)
    n_pairs = n_exp * (n_exp - 1) // 2
    pair = ((lo * (2 * n_exp - 1 - lo)) >> 1) + (hi - lo - 1)
    cls = (g_idx * n_pairs + pair).astype(F32)
    r = lax.broadcasted_iota(jnp.int32, (LANES, ts), 0)
    packed = jnp.where(r == 0, cls, jnp.where(r == 1, w_lo, jnp.where(r == 2, w_hi, 0.0)))
    cls_ref[sub] = packed[:SUBLANES, :]
    xr_ref[rows, half:] = lax.bitcast_convert_type(packed.T, xr_ref.dtype)


def _route_out(t, d, ts):
    return (jax.ShapeDtypeStruct((t, d), F32),
            jax.ShapeDtypeStruct((t, d // 2 + LANES), WORD),
            jax.ShapeDtypeStruct((t // ts, SUBLANES, ts), F32))


def _pool_route_kernel(x_ref, gmix_ref, pw_ref, pscale_ref, gffn_ref, wr_ref, br_ref,
                       h_ref, xr_ref, cls_ref, halo_ref, *, n_grp, n_exp):
    s = pl.program_id(1)
    d = x_ref.shape[1]
    ts = TOKEN_TILE
    c = d // len(POOL_WINDOWS)

    @pl.when(s == 0)
    def _():
        halo_ref[...] = jnp.zeros_like(halo_ref)

    for sub in range(x_ref.shape[0] // ts):
        rows = pl.ds(sub * ts, ts)
        x = x_ref[rows, :]
        xn = x * _rms_scale(x) * gmix_ref[...]
        ext = jnp.concatenate([halo_ref[...], xn], axis=0)
        halo_ref[...] = xn[ts - POOL_HALO:, :]
        pos = (s * x_ref.shape[0] + sub * ts) + lax.broadcasted_iota(jnp.int32, (ts, LANES), 0)
        for gi, w in enumerate(POOL_WINDOWS):
            acc = ext[:, gi * c:(gi + 1) * c]
            k = 1
            while k < w:
                acc = acc + pltpu.roll(acc, k, 0)
                k *= 2
            inv_cnt = 1.0 / jnp.minimum(pos + 1, w).astype(F32)
            inv_cnt = jnp.concatenate([inv_cnt] * (c // LANES), axis=1)
            pooled = acc[POOL_HALO:, :] * inv_cnt - xn[:, gi * c:(gi + 1) * c]
            y = jnp.dot(pooled.astype(BF16), pw_ref[gi], preferred_element_type=F32)
            h_ref[rows, gi * c:(gi + 1) * c] = (x[:, gi * c:(gi + 1) * c]
                                                + y * pscale_ref[:, gi * c:(gi + 1) * c])
        _ffn_norm_route(h_ref[rows, :], gffn_ref, wr_ref, br_ref, xr_ref, cls_ref, sub, n_grp, n_exp)


def _pool_route(x2, gmix, pw, pscale, gffn, wr, br, batch, n_grp, n_exp):
    t, d = x2.shape
    ts, step = TOKEN_TILE, ROUTE_STEP
    seq_steps = t // batch // step
    tok = lambda b, s: (b * seq_steps + s, 0)
    fixed2 = lambda b, s: (0, 0)
    return pl.pallas_call(
        functools.partial(_pool_route_kernel, n_grp=n_grp, n_exp=n_exp),
        grid=(batch, seq_steps),
        in_specs=[pl.BlockSpec((step, d), tok),
                  pl.BlockSpec((1, d), fixed2),
                  pl.BlockSpec(pw.shape, lambda b, s: (0, 0, 0)),
                  pl.BlockSpec((1, d), fixed2),
                  pl.BlockSpec((1, d), fixed2),
                  pl.BlockSpec((d, LANES), fixed2),
                  pl.BlockSpec((1, LANES), fixed2)],
        out_specs=(pl.BlockSpec((step, d), tok),
                   pl.BlockSpec((step, d // 2 + LANES), tok),
                   pl.BlockSpec((step // ts, SUBLANES, ts), lambda b, s: (b * seq_steps + s, 0, 0))),
        out_shape=_route_out(t, d, ts),
        scratch_shapes=[pltpu.VMEM((POOL_HALO, d), F32)],
        compiler_params=_params("arbitrary", "arbitrary"),
        name="pool_route",
    )(x2, gmix, pw, pscale, gffn, wr, br)


def _positions_kernel(cls_ref, pos_ref, tiles_ref, *, tm):
    n_t, _, ts = cls_ref.shape
    ci = lax.broadcasted_iota(jnp.int32, (CLASS_ROWS, ts), 0).astype(F32)

    def onehot(i):
        return (cls_ref[i, 0:1, :] == ci).astype(F32)

    acc = lax.fori_loop(0, n_t, lambda i, a: a + onehot(i), jnp.zeros((CLASS_ROWS, ts), F32))
    counts = jnp.sum(acc, axis=1, keepdims=True)
    tiles = jnp.floor((counts + (tm - 1)) * (1.0 / tm))
    tiles_b = jnp.broadcast_to(tiles, (CLASS_ROWS, LANES))
    tiles_ref[...] = tiles_b
    rr = lax.broadcasted_iota(jnp.int32, (CLASS_ROWS, CLASS_ROWS), 0)
    cc = lax.broadcasted_iota(jnp.int32, (CLASS_ROWS, CLASS_ROWS), 1)
    before = (cc < rr).astype(BF16)
    start = jnp.dot(before, tiles_b.astype(BF16), preferred_element_type=F32)[:, 0:1] * tm
    jj = lax.broadcasted_iota(jnp.int32, (ts, ts), 0)
    ss = lax.broadcasted_iota(jnp.int32, (ts, ts), 1)
    earlier = (jj < ss).astype(BF16)

    def body(i, run):
        oh = onehot(i)
        rank = jnp.dot(oh.astype(BF16), earlier, preferred_element_type=F32)
        pos_ref[i] = jnp.sum(oh * (run + rank), axis=0, keepdims=True).astype(jnp.int32)
        return run + jnp.sum(oh, axis=1, keepdims=True)

    lax.fori_loop(0, n_t, body, start)


def _positions(cls3, tm):
    n_t, _, ts = cls3.shape
    return pl.pallas_call(
        functools.partial(_positions_kernel, tm=tm),
        out_shape=(jax.ShapeDtypeStruct((n_t, 1, ts), jnp.int32),
                   jax.ShapeDtypeStruct((CLASS_ROWS, LANES), F32)),
        compiler_params=pltpu.CompilerParams(vmem_limit_bytes=VMEM_LIMIT),
        name="positions",
    )(cls3)


def _start_rows(n, make):
    for r in range(n):
        make(r).start()


def _dispatch_kernel(tail_ref, pos_ref, xr_ref, xs_hbm, rows_ref, zero_ref, sems, zsem, *, tm, n_tiles):
    i = pl.program_id(0)
    n_steps = pl.num_programs(0)
    n_sub, ts = rows_ref.shape[0], rows_ref.shape[1]
    n_cls = tail_ref.shape[0] - 1

    @pl.when(i == 0)
    def _():
        zero_ref[...] = jnp.zeros_like(zero_ref)
        n_act = tail_ref[n_cls]

        def zero_tile(start):
            return pltpu.make_async_copy(zero_ref, xs_hbm.at[pl.ds(pl.multiple_of(start, tm), tm)], zsem)

        todo = [(tail_ref[c] >= 0, jnp.maximum(tail_ref[c], 0)) for c in range(n_cls)]
        todo += [(n_act + c < n_tiles, jnp.minimum(n_act + c, n_tiles - 1) * tm) for c in range(n_cls)]
        for needed, start in todo:
            @pl.when(needed)
            def _():
                zero_tile(start).start()
        for needed, start in todo:
            @pl.when(needed)
            def _():
                zero_tile(start).wait()

    def wait_rows(s):
        pltpu.make_async_copy(rows_ref.at[s], xs_hbm.at[pl.ds(0, ts)], sems.at[s]).wait()

    for s in range(n_sub):
        @pl.when(i > 0)
        def _():
            wait_rows(s)

        rows_ref[s] = xr_ref[pl.ds(s * ts, ts), :]
        _start_rows(ts, lambda r: pltpu.make_async_copy(
            rows_ref.at[s, pl.ds(r, 1)], xs_hbm.at[pl.ds(pos_ref[0, 0, s * ts + r], 1)], sems.at[s]))

    @pl.when(i == n_steps - 1)
    def _():
        for s in range(n_sub):
            wait_rows(s)


def _dispatch(tail, pos3, xr, n_tiles, tm):
    t, dw = xr.shape
    ts = pos3.shape[2]
    n_sub = PERMUTE_SUBTILES
    pos_step = pos3.reshape(-1, 1, n_sub * ts)
    return pl.pallas_call(
        functools.partial(_dispatch_kernel, tm=tm, n_tiles=n_tiles),
        grid_spec=pltpu.PrefetchScalarGridSpec(
            num_scalar_prefetch=1,
            grid=(pos_step.shape[0],),
            in_specs=[pl.BlockSpec((1, 1, n_sub * ts), lambda i, tail: (i, 0, 0), memory_space=pltpu.SMEM),
                      pl.BlockSpec((n_sub * ts, dw), lambda i, tail: (i, 0))],
            out_specs=pl.BlockSpec(memory_space=pl.ANY),
            scratch_shapes=[pltpu.VMEM((n_sub, ts, dw), WORD),
                            pltpu.VMEM((tm, dw), WORD),
                            pltpu.SemaphoreType.DMA((n_sub,)),
                            pltpu.SemaphoreType.DMA(())]),
        out_shape=jax.ShapeDtypeStruct((n_tiles * tm, dw), WORD),
        compiler_params=_params("arbitrary"),
        name="dispatch",
    )(tail, pos_step, xr)


def _moe_kernel(elo_ref, ehi_ref, nact_ref, xs_ref, wg_lo, wg_hi, wu_lo, wu_hi, wd_lo, wd_hi, ys_ref):
    half = ys_ref.shape[1]
    active = pl.program_id(0) < nact_ref[0]

    @pl.when(active)
    def _():
        x_hi, x_lo = _unpack_pairs(xs_ref[:, :half])
        x = jnp.concatenate([x_hi.astype(BF16), x_lo.astype(BF16)], axis=1)

        def expert(wg, wu, lane):
            g = jnp.dot(x, wg[0], preferred_element_type=F32)
            u = jnp.dot(x, wu[0], preferred_element_type=F32)
            gate = lax.bitcast_convert_type(xs_ref[:, half + lane:half + lane + 1], F32)
            return (g * (1.0 / (1.0 + jnp.exp(-g))) * u * gate).astype(BF16)

        y = (jnp.dot(expert(wg_lo, wu_lo, 1), wd_lo[0], preferred_element_type=F32)
             + jnp.dot(expert(wg_hi, wu_hi, 2), wd_hi[0], preferred_element_type=F32))
        ys_ref[...] = _pack_pairs(y[:, :half], y[:, half:])

    @pl.when(jnp.logical_not(active))
    def _():
        ys_ref[...] = jnp.zeros_like(ys_ref)


def _moe(e_lo, e_hi, n_act, xs, wg, wu, wd, tm):
    n_rows, dw = xs.shape
    half = dw - LANES
    d = 2 * half
    f = wg.shape[2]
    tile = lambda i, lo, hi, na: (jnp.minimum(i, na[0] - 1), 0)
    w_lo = lambda i, lo, hi, na: (lo[i], 0, 0)
    w_hi = lambda i, lo, hi, na: (hi[i], 0, 0)
    return pl.pallas_call(
        _moe_kernel,
        grid_spec=pltpu.PrefetchScalarGridSpec(
            num_scalar_prefetch=3,
            grid=(n_rows // tm,),
            in_specs=[pl.BlockSpec((tm, dw), tile),
                      pl.BlockSpec((1, d, f), w_lo), pl.BlockSpec((1, d, f), w_hi),
                      pl.BlockSpec((1, d, f), w_lo), pl.BlockSpec((1, d, f), w_hi),
                      pl.BlockSpec((1, f, d), w_lo), pl.BlockSpec((1, f, d), w_hi)],
            out_specs=pl.BlockSpec((tm, half), lambda i, lo, hi, na: (i, 0))),
        out_shape=jax.ShapeDtypeStruct((n_rows, half), WORD),
        compiler_params=_params("arbitrary"),
        name="moe",
    )(e_lo, e_hi, n_act, xs, wg, wg, wu, wu, wd, wd)


def _combine_kernel(pos_ref, posn_ref, h_ref, ys_hbm, out_ref, rows_ref, sems):
    i = pl.program_id(0)
    n_steps = pl.num_programs(0)
    n_sub, ts = rows_ref.shape[0], rows_ref.shape[1]

    def gather(p_ref, s):
        _start_rows(ts, lambda r: pltpu.make_async_copy(
            ys_hbm.at[pl.ds(p_ref[0, 0, s * ts + r], 1)], rows_ref.at[s, pl.ds(r, 1)], sems.at[s]))

    @pl.when(i == 0)
    def _():
        for s in range(n_sub):
            gather(pos_ref, s)

    for s in range(n_sub):
        rows = pl.ds(s * ts, ts)
        pltpu.make_async_copy(ys_hbm.at[pl.ds(0, ts)], rows_ref.at[s], sems.at[s]).wait()
        first, second = _unpack_pairs(rows_ref[s])
        out_ref[rows, :] = h_ref[rows, :] + jnp.concatenate([first, second], axis=1)

        @pl.when(i + 1 < n_steps)
        def _():
            gather(posn_ref, s)


def _combine(pos3, h, ys):
    t, d = h.shape
    ts = pos3.shape[2]
    n_sub = PERMUTE_SUBTILES
    pos_step = pos3.reshape(-1, 1, n_sub * ts)
    n_steps = pos_step.shape[0]
    tc = n_sub * ts
    return pl.pallas_call(
        _combine_kernel,
        grid=(n_steps,),
        in_specs=[pl.BlockSpec((1, 1, tc), lambda i: (i, 0, 0), memory_space=pltpu.SMEM),
                  pl.BlockSpec((1, 1, tc), lambda i: (jnp.minimum(i + 1, n_steps - 1), 0, 0),
                               memory_space=pltpu.SMEM),
                  pl.BlockSpec((tc, d), lambda i: (i, 0)),
                  pl.BlockSpec(memory_space=pl.ANY)],
        out_specs=pl.BlockSpec((tc, d), lambda i: (i, 0)),
        out_shape=jax.ShapeDtypeStruct((t, d), F32),
        scratch_shapes=[pltpu.VMEM((n_sub, ts, d // 2), WORD), pltpu.SemaphoreType.DMA((n_sub,))],
        compiler_params=_params("arbitrary"),
        name="combine",
    )(pos_step, pos_step, h, ys)


def _kvq_kernel(h_ref, gkv_ref, gq_ref, w_ref, gain_ref, out_ref, xkv_ref, xq_ref, *, d):
    n = pl.program_id(1)
    bn = out_ref.shape[1]
    k_blocks, kv_blocks = d // bn, 2 * d // bn

    @pl.when(n == 0)
    def _():
        h = h_ref[...]
        xhat = h * _rms_scale(h)
        xkv_ref[...] = (xhat * gkv_ref[...]).astype(BF16)
        xq_ref[...] = (xhat * gq_ref[...]).astype(BF16)

    def project(x_ref, head_norm):
        x = x_ref[...]
        for c0 in range(0, bn, MXU_COLS):
            acc = jnp.dot(x, w_ref[:, c0:c0 + MXU_COLS], preferred_element_type=F32)
            if head_norm:
                for hh in range(MXU_COLS // HEAD_DIM):
                    cols = slice(c0 + hh * HEAD_DIM, c0 + (hh + 1) * HEAD_DIM)
                    blk = acc[:, hh * HEAD_DIM:(hh + 1) * HEAD_DIM]
                    out_ref[:, cols] = (blk * _rms_scale(blk) * gain_ref[:, cols]).astype(BF16)
            else:
                out_ref[:, c0:c0 + MXU_COLS] = acc.astype(BF16)

    @pl.when(n < k_blocks)
    def _():
        project(xkv_ref, True)

    @pl.when((n >= k_blocks) & (n < kv_blocks))
    def _():
        project(xkv_ref, False)

    @pl.when(n >= kv_blocks)
    def _():
        project(xq_ref, True)


def _kvq(h, gkv, gq, w_all, gain_all):
    t, d = h.shape
    n_out = w_all.shape[1]
    tm, bn = PROJ_TILE, PROJ_COLS
    return pl.pallas_call(
        functools.partial(_kvq_kernel, d=d),
        grid=(t // tm, n_out // bn),
        in_specs=[pl.BlockSpec((tm, d), lambda i, n: (i, 0)),
                  pl.BlockSpec((1, d), lambda i, n: (0, 0)),
                  pl.BlockSpec((1, d), lambda i, n: (0, 0)),
                  pl.BlockSpec((d, bn), lambda i, n: (0, n)),
                  pl.BlockSpec((1, bn), lambda i, n: (0, n))],
        out_specs=pl.BlockSpec((tm, bn), lambda i, n: (i, n)),
        out_shape=jax.ShapeDtypeStruct((t, n_out), BF16),
        scratch_shapes=[pltpu.VMEM((tm, d), BF16), pltpu.VMEM((tm, d), BF16)],
        compiler_params=_params("arbitrary", "arbitrary"),
        name="kvq",
    )(h, gkv, gq, w_all, gain_all)


def _attn_kernel(q_ref, k_ref, v_ref, o_ref, acc_ref, carry_ref, logit_ref, mass_ref):
    qi = pl.program_id(2)
    tq = q_ref.shape[0]
    tk = ATTN_KEYS
    n_heads = q_ref.shape[1] // HEAD_DIM
    n_before = (qi * tq) // tk
    jj = lax.broadcasted_iota(jnp.int32, (tk, tk), 0)
    ss = lax.broadcasted_iota(jnp.int32, (tk, tk), 1)
    later = (jj > ss).astype(BF16)
    rows = qi * tq + lax.broadcasted_iota(jnp.int32, (tq, tk), 0)
    cols = n_before * tk + lax.broadcasted_iota(jnp.int32, (tq, tk), 1)
    causal = cols < rows
    sign_bit = jnp.uint32(0x80000000)

    def keys(j):
        return pl.ds(pl.multiple_of(j * tk, tk), tk)

    def score(j, diagonal):
        partial, softplus, first = [], [], []
        for g in range(n_heads):
            hd = slice(g * HEAD_DIM, (g + 1) * HEAD_DIM)
            z = lax.dot_general(q_ref[:, hd], k_ref[keys(j), hd], (((1,), (1,)), ((), ())),
                                preferred_element_type=F32)
            neg_abs = lax.bitcast_convert_type(lax.bitcast_convert_type(z, WORD) | sign_bit, F32)
            sp = jnp.maximum(z, 0.0) + jnp.log2(1.0 + jnp.exp2(neg_abs))
            if diagonal:
                sp = jnp.where(causal, sp, 0.0)
            partial.append(z - sp)
            softplus.append(sp.astype(BF16))
            first.append(sp[:, 0:1])
        after = jnp.dot(jnp.concatenate(softplus, axis=0), later, preferred_element_type=F32)
        for g in range(n_heads):
            after_g = after[g * tq:(g + 1) * tq, :]
            logit = partial[g] - after_g
            if diagonal:
                logit = jnp.where(causal, logit, -jnp.inf)
            logit_ref[g] = logit
            mass_ref[g] = jnp.broadcast_to(first[g] + after_g[:, 0:1], (tq, LANES))

    def accumulate(g, j):
        hd = slice(g * HEAD_DIM, (g + 1) * HEAD_DIM)
        c = carry_ref[g]
        a = jnp.exp2(logit_ref[g] - jnp.concatenate([c] * (tk // LANES), axis=1))
        acc_ref[g] += jnp.dot(a.astype(BF16), v_ref[keys(j), hd], preferred_element_type=F32)
        carry_ref[g] = c + mass_ref[g]

    def alive():
        return jnp.min(carry_ref[...] + mass_ref[...]) < ATTN_DEAD_MASS

    acc_ref[...] = jnp.zeros_like(acc_ref)
    carry_ref[...] = jnp.zeros_like(carry_ref)
    score(n_before, True)

    def body(state):
        i, _ = state
        j = n_before - 1 - i
        for g in range(n_heads):
            accumulate(g, j + 1)
        score(j, False)
        return i + 1, alive()

    n_done, _ = lax.while_loop(lambda state: (state[0] < n_before) & state[1], body,
                               (jnp.int32(0), alive()))
    for g in range(n_heads):
        accumulate(g, n_before - n_done)
        o_ref[:, g * HEAD_DIM:(g + 1) * HEAD_DIM] = acc_ref[g].astype(BF16)


def _attention(kvq, batch, d):
    t = kvq.shape[0]
    seq = t // batch
    gw = ATTN_HEADS * HEAD_DIM
    n_hg = d // gw
    tq = ATTN_TILE
    nq = seq // tq
    return pl.pallas_call(
        _attn_kernel,
        grid=(batch, n_hg, nq),
        in_specs=[pl.BlockSpec((tq, gw), lambda b, h, i: (b * nq + i, 2 * n_hg + h)),
                  pl.BlockSpec((seq, gw), lambda b, h, i: (b, h)),
                  pl.BlockSpec((seq, gw), lambda b, h, i: (b, n_hg + h))],
        out_specs=pl.BlockSpec((tq, gw), lambda b, h, i: (b * nq + i, h)),
        out_shape=jax.ShapeDtypeStruct((t, d), BF16),
        scratch_shapes=[pltpu.VMEM((ATTN_HEADS, tq, HEAD_DIM), F32),
                        pltpu.VMEM((ATTN_HEADS, tq, LANES), F32),
                        pltpu.VMEM((ATTN_HEADS, tq, ATTN_KEYS), F32),
                        pltpu.VMEM((ATTN_HEADS, tq, LANES), F32)],
        compiler_params=_params("arbitrary", "arbitrary", "arbitrary"),
        name="attention",
    )(kvq, kvq, kvq)


def _oproj_route_kernel(o_ref, wo_ref, h_ref, gffn_ref, wr_ref, br_ref,
                        hout_ref, xr_ref, cls_ref, *, n_grp, n_exp):
    ts = TOKEN_TILE
    for sub in range(o_ref.shape[0] // ts):
        rows = pl.ds(sub * ts, ts)
        h = h_ref[rows, :] + jnp.dot(o_ref[rows, :], wo_ref[...], preferred_element_type=F32)
        hout_ref[rows, :] = h
        _ffn_norm_route(h, gffn_ref, wr_ref, br_ref, xr_ref, cls_ref, sub, n_grp, n_exp)


def _oproj_route(o, wo, h, gffn, wr, br, n_grp, n_exp):
    t, d = h.shape
    ts, step = TOKEN_TILE, ROUTE_STEP
    tok = lambda i: (i, 0)
    fixed2 = lambda i: (0, 0)
    return pl.pallas_call(
        functools.partial(_oproj_route_kernel, n_grp=n_grp, n_exp=n_exp),
        grid=(t // step,),
        in_specs=[pl.BlockSpec((step, d), tok),
                  pl.BlockSpec((d, d), fixed2),
                  pl.BlockSpec((step, d), tok),
                  pl.BlockSpec((1, d), fixed2),
                  pl.BlockSpec((d, LANES), fixed2),
                  pl.BlockSpec((1, LANES), fixed2)],
        out_specs=(pl.BlockSpec((step, d), tok),
                   pl.BlockSpec((step, d // 2 + LANES), tok),
                   pl.BlockSpec((step // ts, SUBLANES, ts), lambda i: (i, 0, 0))),
        out_shape=_route_out(t, d, ts),
        compiler_params=_params("arbitrary"),
        name="oproj_route",
    )(o, wo, h, gffn, wr, br)


def _router_weights(w_grp, b_grp, w_rt, b_rt):
    d, n_grp = w_grp.shape
    n_exp = w_rt.shape[2]
    w = jnp.concatenate([w_grp, jnp.transpose(w_rt, (1, 0, 2)).reshape(d, n_grp * n_exp)], axis=1)
    b = jnp.concatenate([b_grp, b_rt.reshape(-1)])
    pad = LANES - w.shape[1]
    return (jnp.pad(w, ((0, 0), (0, pad))).astype(BF16), jnp.pad(b, (0, pad)).reshape(1, LANES))


def _pair_table(n_grp, n_exp):
    lo, hi = [], []
    for g in range(n_grp):
        for a in range(n_exp):
            for b in range(a + 1, n_exp):
                lo.append(g * n_exp + a)
                hi.append(g * n_exp + b)
    return jnp.array(lo, jnp.int32), jnp.array(hi, jnp.int32)


def _moe_layer(h, xr, cls3, wg, wu, wd, first_expert, n_grp, n_exp):
    t, d = h.shape
    tm = MOE_TILE
    cls_lo, cls_hi = _pair_table(n_grp, n_exp)
    n_cls = cls_lo.shape[0]
    n_tiles = t // tm + n_cls
    assert n_cls <= CLASS_ROWS and n_tiles < 256
    pos3, tiles = _positions(cls3, tm)
    tiles_c = tiles[:n_cls, 0].astype(jnp.int32)
    ends = jnp.cumsum(tiles_c)
    n_act = ends[-1]
    tail = jnp.where(tiles_c > 0, (ends - 1) * tm, -1).astype(jnp.int32)
    tile_ids = jnp.minimum(jnp.arange(n_tiles, dtype=jnp.int32), n_act - 1)
    tile_cls = jnp.sum(tile_ids[:, None] >= ends[None, :], axis=1)
    xs = _dispatch(jnp.concatenate([tail, n_act.reshape(1)]), pos3, xr, n_tiles, tm)
    ys = _moe(first_expert + cls_lo[tile_cls], first_expert + cls_hi[tile_cls], n_act.reshape(1),
              xs, wg, wu, wd, tm)
    return _combine(pos3, h, ys)


def kernel(x, norm_mix, norm_ffn, pool_w, pool_scale, kv_norm, w_kv, k_norm, w_q, q_norm, w_o,
           moe_w_grp, moe_b_grp, moe_w_rt, moe_b_rt, moe_w_gate, moe_w_up, moe_w_down):
    batch, seq, d = x.shape
    t = batch * seq
    n_grp, n_exp = moe_w_rt.shape[1], moe_w_rt.shape[3]
    n_heads = d // HEAD_DIM
    assert len(POOL_WINDOWS) == pool_w.shape[1] and seq % ROUTE_STEP == 0 and seq % ATTN_TILE == 0
    assert t % PROJ_TILE == 0 and d % PROJ_COLS == 0 and n_grp + n_grp * n_exp <= LANES
    assert n_heads % ATTN_HEADS == 0 and seq % ATTN_KEYS == 0 and ATTN_KEYS % ATTN_TILE == 0
    row = lambda v: v.reshape(1, -1).astype(F32)

    n_all = moe_w_gate.shape[1]
    f = moe_w_gate.shape[3]
    wg = _cast_bf16([moe_w_gate.reshape(-1, f)]).reshape(-1, d, f)
    wu = _cast_bf16([moe_w_up.reshape(-1, f)]).reshape(-1, d, f)
    wd = _cast_bf16([moe_w_down.reshape(-1, d)]).reshape(-1, f, d)

    wr0, br0 = _router_weights(moe_w_grp[0], moe_b_grp[0], moe_w_rt[0], moe_b_rt[0])
    h1, xr, cls3 = _pool_route(x.reshape(t, d), row(norm_mix[0]), pool_w[0].astype(BF16),
                               row(pool_scale[0]), row(norm_ffn[0]), wr0, br0, batch, n_grp, n_exp)
    h2 = _moe_layer(h1, xr, cls3, wg, wu, wd, 0, n_grp, n_exp)

    w_all = _cast_bf16([w_kv, w_q[0]])
    q_gain = q_norm[0] * (HEAD_DIM ** -0.5 * LOG2E)
    gain_all = jnp.concatenate([jnp.tile(k_norm, n_heads), jnp.ones((d,), F32), jnp.tile(q_gain, n_heads)])
    kvq = _kvq(h2, row(kv_norm), row(norm_mix[1]), w_all, row(gain_all))
    o = _attention(kvq, batch, d)

    wr1, br1 = _router_weights(moe_w_grp[1], moe_b_grp[1], moe_w_rt[1], moe_b_rt[1])
    h3, xr, cls3 = _oproj_route(o, _cast_bf16([w_o[0]]), h2, row(norm_ffn[1]), wr1, br1, n_grp, n_exp)
    h4 = _moe_layer(h3, xr, cls3, wg, wu, wd, n_all, n_grp, n_exp)
    return h4.reshape(batch, seq, d)
```

```python
import functools

import jax
import jax.numpy as jnp
from jax import lax
from jax.experimental import pallas as pl
from jax.experimental.pallas import tpu as pltpu

F32 = jnp.float32
BF16 = jnp.bfloat16
WORD = jnp.uint32

EPS = 1e-6
POOL_WINDOWS = (2, 4, 8, 16)
POOL_HALO = 16
HEAD_DIM = 128
LANES = 128
SUBLANES = 8
MXU_COLS = 256
LOG2E = 1.4426950408889634

TOKEN_TILE = 256
ROUTE_STEP = 512
MOE_TILE = 256
PROJ_TILE = 1024
PROJ_COLS = 1024
ATTN_TILE = 256
ATTN_KEYS = 256
ATTN_DEAD_MASS = 160.0
ATTN_HEADS = 8
PERMUTE_SUBTILES = 2
CLASS_ROWS = 32
CAST_BLOCK_ELEMS = 2 * 1024 * 1024
VMEM_LIMIT = 56 * 1024 * 1024


def _params(*sem):
    return pltpu.CompilerParams(dimension_semantics=sem, vmem_limit_bytes=VMEM_LIMIT)


def _rms_scale(x):
    return lax.rsqrt(jnp.mean(x * x, axis=-1, keepdims=True) + EPS)


def _pack_pairs(first, second):
    return pltpu.pack_elementwise([first, second], packed_dtype=BF16)


def _unpack_pairs(words):
    return tuple(pltpu.unpack_elementwise(words, index=i, packed_dtype=BF16, unpacked_dtype=F32)
                 for i in range(2))


def _cast_kernel(*refs):
    out_ref = refs[-1]
    c0 = 0
    for ref in refs[:-1]:
        out_ref[:, c0:c0 + ref.shape[1]] = ref[...].astype(BF16)
        c0 += ref.shape[1]


def _cast_bf16(parts):
    rows = parts[0].shape[0]
    cols = sum(p.shape[1] for p in parts)
    rb = rows
    while rb * cols > CAST_BLOCK_ELEMS and rb % 32 == 0:
        rb //= 2
    return pl.pallas_call(
        _cast_kernel,
        grid=(rows // rb,),
        in_specs=[pl.BlockSpec((rb, p.shape[1]), lambda i: (i, 0)) for p in parts],
        out_specs=pl.BlockSpec((rb, cols), lambda i: (i, 0)),
        out_shape=jax.ShapeDtypeStruct((rows, cols), BF16),
        compiler_params=_params("arbitrary"),
        name="cast_bf16",
    )(*parts)


def _first_argmax(vals):
    best, idx = vals[0], jnp.zeros(vals[0].shape, jnp.int32)
    for i in range(1, len(vals)):
        upd = vals[i] > best
        best = jnp.where(upd, vals[i], best)
        idx = jnp.where(upd, i, idx)
    return best, idx


def _ffn_norm_route(h, gffn_ref, wr_ref, br_ref, xr_ref, cls_ref, sub, n_grp, n_exp):
    ts, d = h.shape
    half = d // 2
    rows = pl.ds(sub * ts, ts)
    xn = h * _rms_scale(h) * gffn_ref[...]
    xr_ref[rows, :half] = _pack_pairs(xn[:, :half], xn[:, half:])
    logits = jnp.dot(xn.astype(BF16), wr_ref[...], preferred_element_type=F32) + br_ref[...]
    lt = logits.T
    row = lambda i: lt[i:i + 1, :]
    g_best, g_idx = _first_argmax([row(g) for g in range(n_grp)])
    denom = sum(jnp.exp(row(g) - g_best) for g in range(n_grp))
    g_w = 1.0 / denom
    sel = []
    for e in range(n_exp):
        v = row(n_grp + e)
        for g in range(1, n_grp):
            v = jnp.where(g_idx == g, row(n_grp + g * n_exp + e), v)
        sel.append(v)
    v1, i1 = _first_argmax(sel)
    v2 = jnp.full_like(v1, -jnp.inf)
    i2 = jnp.zeros_like(i1)
    for e in range(n_exp):
        upd = (i1 != e) & (sel[e] > v2)
        v2 = jnp.where(upd, sel[e], v2)
        i2 = jnp.where(upd, e, i2)
    t = jnp.exp(v2 - v1)
    w1 = g_w / (1.0 + t)
    w2 = w1 * t
    first_low = i1 < i2
    lo = jnp.where(first_low, i1, i2)
    hi = jnp.where(first_low, i2, i1)
    w_lo = jnp.where(first_low, w1, w2)
    w_hi = jnp.where(first_low, w2, w1)
    n_pairs = n_exp * (n_exp - 1) // 2
    pair = ((lo * (2 * n_exp - 1 - lo)) >> 1) + (hi - lo - 1)
    cls = (g_idx * n_pairs + pair).astype(F32)
    r = lax.broadcasted_iota(jnp.int32, (LANES, ts), 0)
    packed = jnp.where(r == 0, cls, jnp.where(r == 1, w_lo, jnp.where(r == 2, w_hi, 0.0)))
    cls_ref[sub] = packed[:SUBLANES, :]
    xr_ref[rows, half:] = lax.bitcast_convert_type(packed.T, xr_ref.dtype)


def _route_out(t, d, ts):
    return (jax.ShapeDtypeStruct((t, d), F32),
            jax.ShapeDtypeStruct((t, d // 2 + LANES), WORD),
            jax.ShapeDtypeStruct((t // ts, SUBLANES, ts), F32))


def _pool_route_kernel(x_ref, gmix_ref, pw_ref, pscale_ref, gffn_ref, wr_ref, br_ref,
                       h_ref, xr_ref, cls_ref, halo_ref, *, n_grp, n_exp):
    s = pl.program_id(1)
    d = x_ref.shape[1]
    ts = TOKEN_TILE
    c = d // len(POOL_WINDOWS)

    @pl.when(s == 0)
    def _():
        halo_ref[...] = jnp.zeros_like(halo_ref)

    for sub in range(x_ref.shape[0] // ts):
        rows = pl.ds(sub * ts, ts)
        x = x_ref[rows, :]
        xn = x * _rms_scale(x) * gmix_ref[...]
        ext = jnp.concatenate([halo_ref[...], xn], axis=0)
        halo_ref[...] = xn[ts - POOL_HALO:, :]
        pos = (s * x_ref.shape[0] + sub * ts) + lax.broadcasted_iota(jnp.int32, (ts, LANES), 0)
        for gi, w in enumerate(POOL_WINDOWS):
            acc = ext[:, gi * c:(gi + 1) * c]
            k = 1
            while k < w:
                acc = acc + pltpu.roll(acc, k, 0)
                k *= 2
            inv_cnt = 1.0 / jnp.minimum(pos + 1, w).astype(F32)
            inv_cnt = jnp.concatenate([inv_cnt] * (c // LANES), axis=1)
            pooled = acc[POOL_HALO:, :] * inv_cnt - xn[:, gi * c:(gi + 1) * c]
            y = jnp.dot(pooled.astype(BF16), pw_ref[gi], preferred_element_type=F32)
            h_ref[rows, gi * c:(gi + 1) * c] = (x[:, gi * c:(gi + 1) * c]
                                                + y * pscale_ref[:, gi * c:(gi + 1) * c])
        _ffn_norm_route(h_ref[rows, :], gffn_ref, wr_ref, br_ref, xr_ref, cls_ref, sub, n_grp, n_exp)


def _pool_route(x2, gmix, pw, pscale, gffn, wr, br, batch, n_grp, n_exp):
    t, d = x2.shape
    ts, step = TOKEN_TILE, ROUTE_STEP
    seq_steps = t // batch // step
    tok = lambda b, s: (b * seq_steps + s, 0)
    fixed2 = lambda b, s: (0, 0)
    return pl.pallas_call(
        functools.partial(_pool_route_kernel, n_grp=n_grp, n_exp=n_exp),
        grid=(batch, seq_steps),
        in_specs=[pl.BlockSpec((step, d), tok),
                  pl.BlockSpec((1, d), fixed2),
                  pl.BlockSpec(pw.shape, lambda b, s: (0, 0, 0)),
                  pl.BlockSpec((1, d), fixed2),
                  pl.BlockSpec((1, d), fixed2),
                  pl.BlockSpec((d, LANES), fixed2),
                  pl.BlockSpec((1, LANES), fixed2)],
        out_specs=(pl.BlockSpec((step, d), tok),
                   pl.BlockSpec((step, d // 2 + LANES), tok),
                   pl.BlockSpec((step // ts, SUBLANES, ts), lambda b, s: (b * seq_steps + s, 0, 0))),
        out_shape=_route_out(t, d, ts),
        scratch_shapes=[pltpu.VMEM((POOL_HALO, d), F32)],
        compiler_params=_params("arbitrary", "arbitrary"),
        name="pool_route",
    )(x2, gmix, pw, pscale, gffn, wr, br)


def _positions_kernel(cls_ref, pos_ref, tiles_ref, *, tm):
    n_t, _, ts = cls_ref.shape
    ci = lax.broadcasted_iota(jnp.int32, (CLASS_ROWS, ts), 0).astype(F32)

    def onehot(i):
        return (cls_ref[i, 0:1, :] == ci).astype(F32)

    acc = lax.fori_loop(0, n_t, lambda i, a: a + onehot(i), jnp.zeros((CLASS_ROWS, ts), F32))
    counts = jnp.sum(acc, axis=1, keepdims=True)
    tiles = jnp.floor((counts + (tm - 1)) * (1.0 / tm))
    tiles_b = jnp.broadcast_to(tiles, (CLASS_ROWS, LANES))
    tiles_ref[...] = tiles_b
    rr = lax.broadcasted_iota(jnp.int32, (CLASS_ROWS, CLASS_ROWS), 0)
    cc = lax.broadcasted_iota(jnp.int32, (CLASS_ROWS, CLASS_ROWS), 1)
    before = (cc < rr).astype(BF16)
    start = jnp.dot(before, tiles_b.astype(BF16), preferred_element_type=F32)[:, 0:1] * tm
    jj = lax.broadcasted_iota(jnp.int32, (ts, ts), 0)
    ss = lax.broadcasted_iota(jnp.int32, (ts, ts), 1)
    earlier = (jj < ss).astype(BF16)

    def body(i, run):
        oh = onehot(i)
        rank = jnp.dot(oh.astype(BF16), earlier, preferred_element_type=F32)
        pos_ref[i] = jnp.sum(oh * (run + rank), axis=0, keepdims=True).astype(jnp.int32)
        return run + jnp.sum(oh, axis=1, keepdims=True)

    lax.fori_loop(0, n_t, body, start)


def _positions(cls3, tm):
    n_t, _, ts = cls3.shape
    return pl.pallas_call(
        functools.partial(_positions_kernel, tm=tm),
        out_shape=(jax.ShapeDtypeStruct((n_t, 1, ts), jnp.int32),
                   jax.ShapeDtypeStruct((CLASS_ROWS, LANES), F32)),
        compiler_params=pltpu.CompilerParams(vmem_limit_bytes=VMEM_LIMIT),
        name="positions",
    )(cls3)


def _start_rows(n, make):
    for r in range(n):
        make(r).start()


def _dispatch_kernel(tail_ref, pos_ref, xr_ref, xs_hbm, rows_ref, zero_ref, sems, zsem, *, tm, n_tiles):
    i = pl.program_id(0)
    n_steps = pl.num_programs(0)
    n_sub, ts = rows_ref.shape[0], rows_ref.shape[1]
    n_cls = tail_ref.shape[0] - 1

    @pl.when(i == 0)
    def _():
        zero_ref[...] = jnp.zeros_like(zero_ref)
        n_act = tail_ref[n_cls]

        def zero_tile(start):
            return pltpu.make_async_copy(zero_ref, xs_hbm.at[pl.ds(pl.multiple_of(start, tm), tm)], zsem)

        todo = [(tail_ref[c] >= 0, jnp.maximum(tail_ref[c], 0)) for c in range(n_cls)]
        todo += [(n_act + c < n_tiles, jnp.minimum(n_act + c, n_tiles - 1) * tm) for c in range(n_cls)]
        for needed, start in todo:
            @pl.when(needed)
            def _():
                zero_tile(start).start()
        for needed, start in todo:
            @pl.when(needed)
            def _():
                zero_tile(start).wait()

    def wait_rows(s):
        pltpu.make_async_copy(rows_ref.at[s], xs_hbm.at[pl.ds(0, ts)], sems.at[s]).wait()

    for s in range(n_sub):
        @pl.when(i > 0)
        def _():
            wait_rows(s)

        rows_ref[s] = xr_ref[pl.ds(s * ts, ts), :]
        _start_rows(ts, lambda r: pltpu.make_async_copy(
            rows_ref.at[s, pl.ds(r, 1)], xs_hbm.at[pl.ds(pos_ref[0, 0, s * ts + r], 1)], sems.at[s]))

    @pl.when(i == n_steps - 1)
    def _():
        for s in range(n_sub):
            wait_rows(s)


def _dispatch(tail, pos3, xr, n_tiles, tm):
    t, dw = xr.shape
    ts = pos3.shape[2]
    n_sub = PERMUTE_SUBTILES
    pos_step = pos3.reshape(-1, 1, n_sub * ts)
    return pl.pallas_call(
        functools.partial(_dispatch_kernel, tm=tm, n_tiles=n_tiles),
        grid_spec=pltpu.PrefetchScalarGridSpec(
            num_scalar_prefetch=1,
            grid=(pos_step.shape[0],),
            in_specs=[pl.BlockSpec((1, 1, n_sub * ts), lambda i, tail: (i, 0, 0), memory_space=pltpu.SMEM),
                      pl.BlockSpec((n_sub * ts, dw), lambda i, tail: (i, 0))],
            out_specs=pl.BlockSpec(memory_space=pl.ANY),
            scratch_shapes=[pltpu.VMEM((n_sub, ts, dw), WORD),
                            pltpu.VMEM((tm, dw), WORD),
                            pltpu.SemaphoreType.DMA((n_sub,)),
                            pltpu.SemaphoreType.DMA(())]),
        out_shape=jax.ShapeDtypeStruct((n_tiles * tm, dw), WORD),
        compiler_params=_params("arbitrary"),
        name="dispatch",
    )(tail, pos_step, xr)


def _moe_kernel(elo_ref, ehi_ref, nact_ref, xs_ref, wg_lo, wg_hi, wu_lo, wu_hi, wd_lo, wd_hi, ys_ref):
    half = ys_ref.shape[1]
    active = pl.program_id(0) < nact_ref[0]

    @pl.when(active)
    def _():
        x_hi, x_lo = _unpack_pairs(xs_ref[:, :half])
        x = jnp.concatenate([x_hi.astype(BF16), x_lo.astype(BF16)], axis=1)

        def expert(wg, wu, lane):
            g = jnp.dot(x, wg[0], preferred_element_type=F32)
            u = jnp.dot(x, wu[0], preferred_element_type=F32)
            gate = lax.bitcast_convert_type(xs_ref[:, half + lane:half + lane + 1], F32)
            return (g * (1.0 / (1.0 + jnp.exp(-g))) * u * gate).astype(BF16)

        y = (jnp.dot(expert(wg_lo, wu_lo, 1), wd_lo[0], preferred_element_type=F32)
             + jnp.dot(expert(wg_hi, wu_hi, 2), wd_hi[0], preferred_element_type=F32))
        ys_ref[...] = _pack_pairs(y[:, :half], y[:, half:])

    @pl.when(jnp.logical_not(active))
    def _():
        ys_ref[...] = jnp.zeros_like(ys_ref)


def _moe(e_lo, e_hi, n_act, xs, wg, wu, wd, tm):
    n_rows, dw = xs.shape
    half = dw - LANES
    d = 2 * half
    f = wg.shape[2]
    tile = lambda i, lo, hi, na: (jnp.minimum(i, na[0] - 1), 0)
    w_lo = lambda i, lo, hi, na: (lo[i], 0, 0)
    w_hi = lambda i, lo, hi, na: (hi[i], 0, 0)
    return pl.pallas_call(
        _moe_kernel,
        grid_spec=pltpu.PrefetchScalarGridSpec(
            num_scalar_prefetch=3,
            grid=(n_rows // tm,),
            in_specs=[pl.BlockSpec((tm, dw), tile),
                      pl.BlockSpec((1, d, f), w_lo), pl.BlockSpec((1, d, f), w_hi),
                      pl.BlockSpec((1, d, f), w_lo), pl.BlockSpec((1, d, f), w_hi),
                      pl.BlockSpec((1, f, d), w_lo), pl.BlockSpec((1, f, d), w_hi)],
            out_specs=pl.BlockSpec((tm, half), lambda i, lo, hi, na: (i, 0))),
        out_shape=jax.ShapeDtypeStruct((n_rows, half), WORD),
        compiler_params=_params("arbitrary"),
        name="moe",
    )(e_lo, e_hi, n_act, xs, wg, wg, wu, wu, wd, wd)


def _combine_kernel(pos_ref, posn_ref, h_ref, ys_hbm, out_ref, rows_ref, sems):
    i = pl.program_id(0)
    n_steps = pl.num_programs(0)
    n_sub, ts = rows_ref.shape[0], rows_ref.shape[1]

    def gather(p_ref, s):
        _start_rows(ts, lambda r: pltpu.make_async_copy(
            ys_hbm.at[pl.ds(p_ref[0, 0, s * ts + r], 1)], rows_ref.at[s, pl.ds(r, 1)], sems.at[s]))

    @pl.when(i == 0)
    def _():
        for s in range(n_sub):
            gather(pos_ref, s)

    for s in range(n_sub):
        rows = pl.ds(s * ts, ts)
        pltpu.make_async_copy(ys_hbm.at[pl.ds(0, ts)], rows_ref.at[s], sems.at[s]).wait()
        first, second = _unpack_pairs(rows_ref[s])
        out_ref[rows, :] = h_ref[rows, :] + jnp.concatenate([first, second], axis=1)

        @pl.when(i + 1 < n_steps)
        def _():
            gather(posn_ref, s)


def _combine(pos3, h, ys):
    t, d = h.shape
    ts = pos3.shape[2]
    n_sub = PERMUTE_SUBTILES
    pos_step = pos3.reshape(-1, 1, n_sub * ts)
    n_steps = pos_step.shape[0]
    tc = n_sub * ts
    return pl.pallas_call(
        _combine_kernel,
        grid=(n_steps,),
        in_specs=[pl.BlockSpec((1, 1, tc), lambda i: (i, 0, 0), memory_space=pltpu.SMEM),
                  pl.BlockSpec((1, 1, tc), lambda i: (jnp.minimum(i + 1, n_steps - 1), 0, 0),
                               memory_space=pltpu.SMEM),
                  pl.BlockSpec((tc, d), lambda i: (i, 0)),
                  pl.BlockSpec(memory_space=pl.ANY)],
        out_specs=pl.BlockSpec((tc, d), lambda i: (i, 0)),
        out_shape=jax.ShapeDtypeStruct((t, d), F32),
        scratch_shapes=[pltpu.VMEM((n_sub, ts, d // 2), WORD), pltpu.SemaphoreType.DMA((n_sub,))],
        compiler_params=_params("arbitrary"),
        name="combine",
    )(pos_step, pos_step, h, ys)


def _kvq_kernel(h_ref, gkv_ref, gq_ref, w_ref, gain_ref, out_ref, xkv_ref, xq_ref, *, d):
    n = pl.program_id(1)
    bn = out_ref.shape[1]
    k_blocks, kv_blocks = d // bn, 2 * d // bn

    @pl.when(n == 0)
    def _():
        h = h_ref[...]
        xhat = h * _rms_scale(h)
        xkv_ref[...] = (xhat * gkv_ref[...]).astype(BF16)
        xq_ref[...] = (xhat * gq_ref[...]).astype(BF16)

    def project(x_ref, head_norm):
        x = x_ref[...]
        for c0 in range(0, bn, MXU_COLS):
            acc = jnp.dot(x, w_ref[:, c0:c0 + MXU_COLS], preferred_element_type=F32)
            if head_norm:
                for hh in range(MXU_COLS // HEAD_DIM):
                    cols = slice(c0 + hh * HEAD_DIM, c0 + (hh + 1) * HEAD_DIM)
                    blk = acc[:, hh * HEAD_DIM:(hh + 1) * HEAD_DIM]
                    out_ref[:, cols] = (blk * _rms_scale(blk) * gain_ref[:, cols]).astype(BF16)
            else:
                out_ref[:, c0:c0 + MXU_COLS] = acc.astype(BF16)

    @pl.when(n < k_blocks)
    def _():
        project(xkv_ref, True)

    @pl.when((n >= k_blocks) & (n < kv_blocks))
    def _():
        project(xkv_ref, False)

    @pl.when(n >= kv_blocks)
    def _():
        project(xq_ref, True)


def _kvq(h, gkv, gq, w_all, gain_all):
    t, d = h.shape
    n_out = w_all.shape[1]
    tm, bn = PROJ_TILE, PROJ_COLS
    return pl.pallas_call(
        functools.partial(_kvq_kernel, d=d),
        grid=(t // tm, n_out // bn),
        in_specs=[pl.BlockSpec((tm, d), lambda i, n: (i, 0)),
                  pl.BlockSpec((1, d), lambda i, n: (0, 0)),
                  pl.BlockSpec((1, d), lambda i, n: (0, 0)),
                  pl.BlockSpec((d, bn), lambda i, n: (0, n)),
                  pl.BlockSpec((1, bn), lambda i, n: (0, n))],
        out_specs=pl.BlockSpec((tm, bn), lambda i, n: (i, n)),
        out_shape=jax.ShapeDtypeStruct((t, n_out), BF16),
        scratch_shapes=[pltpu.VMEM((tm, d), BF16), pltpu.VMEM((tm, d), BF16)],
        compiler_params=_params("arbitrary", "arbitrary"),
        name="kvq",
    )(h, gkv, gq, w_all, gain_all)


def _attn_kernel(q_ref, k_ref, v_ref, o_ref, acc_ref, carry_ref, logit_ref, mass_ref):
    qi = pl.program_id(2)
    tq = q_ref.shape[0]
    tk = ATTN_KEYS
    n_heads = q_ref.shape[1] // HEAD_DIM
    n_before = (qi * tq) // tk
    jj = lax.broadcasted_iota(jnp.int32, (tk, tk), 0)
    ss = lax.broadcasted_iota(jnp.int32, (tk, tk), 1)
    later = (jj > ss).astype(BF16)
    rows = qi * tq + lax.broadcasted_iota(jnp.int32, (tq, tk), 0)
    cols = n_before * tk + lax.broadcasted_iota(jnp.int32, (tq, tk), 1)
    causal = cols < rows
    sign_bit = jnp.uint32(0x80000000)

    def keys(j):
        return pl.ds(pl.multiple_of(j * tk, tk), tk)

    def score(j, diagonal):
        partial, softplus, first = [], [], []
        for g in range(n_heads):
            hd = slice(g * HEAD_DIM, (g + 1) * HEAD_DIM)
            z = lax.dot_general(q_ref[:, hd], k_ref[keys(j), hd], (((1,), (1,)), ((), ())),
                                preferred_element_type=F32)
            if diagonal:
                z = jnp.where(causal, z, -jnp.inf)
            neg_abs = lax.bitcast_convert_type(lax.bitcast_convert_type(z, WORD) | sign_bit, F32)
            sp = jnp.maximum(z, 0.0) + jnp.log2(1.0 + jnp.exp2(neg_abs))
            partial.append(z - sp)
            softplus.append(sp.astype(BF16))
            first.append(sp[:, 0:1])
        after = jnp.dot(jnp.concatenate(softplus, axis=0), later, preferred_element_type=F32)
        for g in range(n_heads):
            after_g = after[g * tq:(g + 1) * tq, :]
            logit_ref[g] = partial[g] - after_g
            mass_ref[g] = jnp.broadcast_to(first[g] + after_g[:, 0:1], (tq, LANES))

    def accumulate(g, j):
        hd = slice(g * HEAD_DIM, (g + 1) * HEAD_DIM)
        c = carry_ref[g]
        a = jnp.exp2(logit_ref[g] - jnp.concatenate([c] * (tk // LANES), axis=1))
        acc_ref[g] += jnp.dot(a.astype(BF16), v_ref[keys(j), hd], preferred_element_type=F32)
        carry_ref[g] = c + mass_ref[g]

    def alive():
        return jnp.min(carry_ref[...] + mass_ref[...]) < ATTN_DEAD_MASS

    acc_ref[...] = jnp.zeros_like(acc_ref)
    carry_ref[...] = jnp.zeros_like(carry_ref)
    score(n_before, True)

    def body(state):
        i, _ = state
        j = n_before - 1 - i
        for g in range(n_heads):
            accumulate(g, j + 1)
        score(j, False)
        return i + 1, alive()

    start_alive = jnp.bool_(True) if tq == tk else alive()
    n_done, _ = lax.while_loop(lambda state: (state[0] < n_before) & state[1], body,
                               (jnp.int32(0), start_alive))
    for g in range(n_heads):
        accumulate(g, n_before - n_done)
        o_ref[:, g * HEAD_DIM:(g + 1) * HEAD_DIM] = acc_ref[g].astype(BF16)


def _attention(kvq, batch, d):
    t = kvq.shape[0]
    seq = t // batch
    gw = ATTN_HEADS * HEAD_DIM
    n_hg = d // gw
    tq = ATTN_TILE
    nq = seq // tq
    return pl.pallas_call(
        _attn_kernel,
        grid=(batch, n_hg, nq),
        in_specs=[pl.BlockSpec((tq, gw), lambda b, h, i: (b * nq + i, 2 * n_hg + h)),
                  pl.BlockSpec((seq, gw), lambda b, h, i: (b, h)),
                  pl.BlockSpec((seq, gw), lambda b, h, i: (b, n_hg + h))],
        out_specs=pl.BlockSpec((tq, gw), lambda b, h, i: (b * nq + i, h)),
        out_shape=jax.ShapeDtypeStruct((t, d), BF16),
        scratch_shapes=[pltpu.VMEM((ATTN_HEADS, tq, HEAD_DIM), F32),
                        pltpu.VMEM((ATTN_HEADS, tq, LANES), F32),
                        pltpu.VMEM((ATTN_HEADS, tq, ATTN_KEYS), F32),
                        pltpu.VMEM((ATTN_HEADS, tq, LANES), F32)],
        compiler_params=_params("arbitrary", "arbitrary", "arbitrary"),
        name="attention",
    )(kvq, kvq, kvq)


def _oproj_route_kernel(o_ref, wo_ref, h_ref, gffn_ref, wr_ref, br_ref,
                        hout_ref, xr_ref, cls_ref, *, n_grp, n_exp):
    ts = TOKEN_TILE
    for sub in range(o_ref.shape[0] // ts):
        rows = pl.ds(sub * ts, ts)
        h = h_ref[rows, :] + jnp.dot(o_ref[rows, :], wo_ref[...], preferred_element_type=F32)
        hout_ref[rows, :] = h
        _ffn_norm_route(h, gffn_ref, wr_ref, br_ref, xr_ref, cls_ref, sub, n_grp, n_exp)


def _oproj_route(o, wo, h, gffn, wr, br, n_grp, n_exp):
    t, d = h.shape
    ts, step = TOKEN_TILE, ROUTE_STEP
    tok = lambda i: (i, 0)
    fixed2 = lambda i: (0, 0)
    return pl.pallas_call(
        functools.partial(_oproj_route_kernel, n_grp=n_grp, n_exp=n_exp),
        grid=(t // step,),
        in_specs=[pl.BlockSpec((step, d), tok),
                  pl.BlockSpec((d, d), fixed2),
                  pl.BlockSpec((step, d), tok),
                  pl.BlockSpec((1, d), fixed2),
                  pl.BlockSpec((d, LANES), fixed2),
                  pl.BlockSpec((1, LANES), fixed2)],
        out_specs=(pl.BlockSpec((step, d), tok),
                   pl.BlockSpec((step, d // 2 + LANES), tok),
                   pl.BlockSpec((step // ts, SUBLANES, ts), lambda i: (i, 0, 0))),
        out_shape=_route_out(t, d, ts),
        compiler_params=_params("arbitrary"),
        name="oproj_route",
    )(o, wo, h, gffn, wr, br)


def _router_weights(w_grp, b_grp, w_rt, b_rt):
    d, n_grp = w_grp.shape
    n_exp = w_rt.shape[2]
    w = jnp.concatenate([w_grp, jnp.transpose(w_rt, (1, 0, 2)).reshape(d, n_grp * n_exp)], axis=1)
    b = jnp.concatenate([b_grp, b_rt.reshape(-1)])
    pad = LANES - w.shape[1]
    return (jnp.pad(w, ((0, 0), (0, pad))).astype(BF16), jnp.pad(b, (0, pad)).reshape(1, LANES))


def _pair_table(n_grp, n_exp):
    lo, hi = [], []
    for g in range(n_grp):
        for a in range(n_exp):
            for b in range(a + 1, n_exp):
                lo.append(g * n_exp + a)
                hi.append(g * n_exp + b)
    return jnp.array(lo, jnp.int32), jnp.array(hi, jnp.int32)


def _moe_layer(h, xr, cls3, wg, wu, wd, first_expert, n_grp, n_exp):
    t, d = h.shape
    tm = MOE_TILE
    cls_lo, cls_hi = _pair_table(n_grp, n_exp)
    n_cls = cls_lo.shape[0]
    n_tiles = t // tm + n_cls
    assert n_cls <= CLASS_ROWS and n_tiles < 256
    pos3, tiles = _positions(cls3, tm)
    tiles_c = tiles[:n_cls, 0].astype(jnp.int32)
    ends = jnp.cumsum(tiles_c)
    n_act = ends[-1]
    tail = jnp.where(tiles_c > 0, (ends - 1) * tm, -1).astype(jnp.int32)
    tile_ids = jnp.minimum(jnp.arange(n_tiles, dtype=jnp.int32), n_act - 1)
    tile_cls = jnp.sum(tile_ids[:, None] >= ends[None, :], axis=1)
    xs = _dispatch(jnp.concatenate([tail, n_act.reshape(1)]), pos3, xr, n_tiles, tm)
    ys = _moe(first_expert + cls_lo[tile_cls], first_expert + cls_hi[tile_cls], n_act.reshape(1),
              xs, wg, wu, wd, tm)
    return _combine(pos3, h, ys)


def kernel(x, norm_mix, norm_ffn, pool_w, pool_scale, kv_norm, w_kv, k_norm, w_q, q_norm, w_o,
           moe_w_grp, moe_b_grp, moe_w_rt, moe_b_rt, moe_w_gate, moe_w_up, moe_w_down):
    batch, seq, d = x.shape
    t = batch * seq
    n_grp, n_exp = moe_w_rt.shape[1], moe_w_rt.shape[3]
    n_heads = d // HEAD_DIM
    assert len(POOL_WINDOWS) == pool_w.shape[1] and seq % ROUTE_STEP == 0 and seq % ATTN_TILE == 0
    assert t % PROJ_TILE == 0 and d % PROJ_COLS == 0 and n_grp + n_grp * n_exp <= LANES
    assert n_heads % ATTN_HEADS == 0 and seq % ATTN_KEYS == 0 and ATTN_KEYS % ATTN_TILE == 0
    row = lambda v: v.reshape(1, -1).astype(F32)

    n_all = moe_w_gate.shape[1]
    f = moe_w_gate.shape[3]
    wg = _cast_bf16([moe_w_gate.reshape(-1, f)]).reshape(-1, d, f)
    wu = _cast_bf16([moe_w_up.reshape(-1, f)]).reshape(-1, d, f)
    wd = _cast_bf16([moe_w_down.reshape(-1, d)]).reshape(-1, f, d)

    wr0, br0 = _router_weights(moe_w_grp[0], moe_b_grp[0], moe_w_rt[0], moe_b_rt[0])
    h1, xr, cls3 = _pool_route(x.reshape(t, d), row(norm_mix[0]), pool_w[0].astype(BF16),
                               row(pool_scale[0]), row(norm_ffn[0]), wr0, br0, batch, n_grp, n_exp)
    h2 = _moe_layer(h1, xr, cls3, wg, wu, wd, 0, n_grp, n_exp)

    w_all = _cast_bf16([w_kv, w_q[0]])
    q_gain = q_norm[0] * (HEAD_DIM ** -0.5 * LOG2E)
    gain_all = jnp.concatenate([jnp.tile(k_norm, n_heads), jnp.ones((d,), F32), jnp.tile(q_gain, n_heads)])
    kvq = _kvq(h2, row(kv_norm), row(norm_mix[1]), w_all, row(gain_all))
    o = _attention(kvq, batch, d)

    wr1, br1 = _router_weights(moe_w_grp[1], moe_b_grp[1], moe_w_rt[1], moe_b_rt[1])
    h3, xr, cls3 = _oproj_route(o, _cast_bf16([w_o[0]]), h2, row(norm_ffn[1]), wr1, br1, n_grp, n_exp)
    h4 = _moe_layer(h3, xr, cls3, wg, wu, wd, n_all, n_grp, n_exp)
    return h4.reshape(batch, seq, d)
```

```python
import functools

import jax
import jax.numpy as jnp
from jax import lax
from jax.experimental import pallas as pl
from jax.experimental.pallas import tpu as pltpu

F32 = jnp.float32
BF16 = jnp.bfloat16
WORD = jnp.uint32

EPS = 1e-6
POOL_WINDOWS = (2, 4, 8, 16)
POOL_HALO = 16
HEAD_DIM = 128
LANES = 128
SUBLANES = 8
MXU_COLS = 256
LOG2E = 1.4426950408889634

TOKEN_TILE = 256
ROUTE_STEP = 512
MOE_TILE = 256
PROJ_TILE = 1024
PROJ_COLS = 1024
ATTN_TILE = 256
ATTN_KEYS = 256
ATTN_DEAD_MASS = 160.0
ATTN_HEADS = 8
PERMUTE_SUBTILES = 4
CLASS_ROWS = 32
CAST_BLOCK_ELEMS = 2 * 1024 * 1024
VMEM_LIMIT = 56 * 1024 * 1024


def _params(*sem):
    return pltpu.CompilerParams(dimension_semantics=sem, vmem_limit_bytes=VMEM_LIMIT)


def _rms_scale(x):
    return lax.rsqrt(jnp.mean(x * x, axis=-1, keepdims=True) + EPS)


def _pack_pairs(first, second):
    return pltpu.pack_elementwise([first, second], packed_dtype=BF16)


def _unpack_pairs(words):
    return tuple(pltpu.unpack_elementwise(words, index=i, packed_dtype=BF16, unpacked_dtype=F32)
                 for i in range(2))


def _cast_kernel(*refs):
    out_ref = refs[-1]
    c0 = 0
    for ref in refs[:-1]:
        out_ref[:, c0:c0 + ref.shape[1]] = ref[...].astype(BF16)
        c0 += ref.shape[1]


def _cast_bf16(parts):
    rows = parts[0].shape[0]
    cols = sum(p.shape[1] for p in parts)
    rb = rows
    while rb * cols > CAST_BLOCK_ELEMS and rb % 32 == 0:
        rb //= 2
    return pl.pallas_call(
        _cast_kernel,
        grid=(rows // rb,),
        in_specs=[pl.BlockSpec((rb, p.shape[1]), lambda i: (i, 0)) for p in parts],
        out_specs=pl.BlockSpec((rb, cols), lambda i: (i, 0)),
        out_shape=jax.ShapeDtypeStruct((rows, cols), BF16),
        compiler_params=_params("arbitrary"),
        name="cast_bf16",
    )(*parts)


def _first_argmax(vals):
    best, idx = vals[0], jnp.zeros(vals[0].shape, jnp.int32)
    for i in range(1, len(vals)):
        upd = vals[i] > best
        best = jnp.where(upd, vals[i], best)
        idx = jnp.where(upd, i, idx)
    return best, idx


def _ffn_norm_route(h, gffn_ref, wr_ref, br_ref, xr_ref, cls_ref, sub, n_grp, n_exp):
    ts, d = h.shape
    half = d // 2
    rows = pl.ds(sub * ts, ts)
    xn = h * _rms_scale(h) * gffn_ref[...]
    xr_ref[rows, :half] = _pack_pairs(xn[:, :half], xn[:, half:])
    logits = jnp.dot(xn.astype(BF16), wr_ref[...], preferred_element_type=F32) + br_ref[...]
    lt = logits.T
    row = lambda i: lt[i:i + 1, :]
    g_best, g_idx = _first_argmax([row(g) for g in range(n_grp)])
    denom = sum(jnp.exp(row(g) - g_best) for g in range(n_grp))
    g_w = 1.0 / denom
    sel = []
    for e in range(n_exp):
        v = row(n_grp + e)
        for g in range(1, n_grp):
            v = jnp.where(g_idx == g, row(n_grp + g * n_exp + e), v)
        sel.append(v)
    v1, i1 = _first_argmax(sel)
    v2 = jnp.full_like(v1, -jnp.inf)
    i2 = jnp.zeros_like(i1)
    for e in range(n_exp):
        upd = (i1 != e) & (sel[e] > v2)
        v2 = jnp.where(upd, sel[e], v2)
        i2 = jnp.where(upd, e, i2)
    t = jnp.exp(v2 - v1)
    w1 = g_w / (1.0 + t)
    w2 = w1 * t
    first_low = i1 < i2
    lo = jnp.where(first_low, i1, i2)
    hi = jnp.where(first_low, i2, i1)
    w_lo = jnp.where(first_low, w1, w2)
    w_hi = jnp.where(first_low, w2, w1)
    n_pairs = n_exp * (n_exp - 1) // 2
    pair = ((lo * (2 * n_exp - 1 - lo)) >> 1) + (hi - lo - 1)
    cls = (g_idx * n_pairs + pair).astype(F32)
    r = lax.broadcasted_iota(jnp.int32, (LANES, ts), 0)
    packed = jnp.where(r == 0, cls, jnp.where(r == 1, w_lo, jnp.where(r == 2, w_hi, 0.0)))
    cls_ref[sub] = packed[:SUBLANES, :]
    xr_ref[rows, half:] = lax.bitcast_convert_type(packed.T, xr_ref.dtype)


def _route_out(t, d, ts):
    return (jax.ShapeDtypeStruct((t, d), F32),
            jax.ShapeDtypeStruct((t, d // 2 + LANES), WORD),
            jax.ShapeDtypeStruct((t // ts, SUBLANES, ts), F32))


def _pool_route_kernel(x_ref, gmix_ref, pw_ref, pscale_ref, gffn_ref, wr_ref, br_ref,
                       h_ref, xr_ref, cls_ref, halo_ref, *, n_grp, n_exp):
    s = pl.program_id(1)
    d = x_ref.shape[1]
    ts = TOKEN_TILE
    c = d // len(POOL_WINDOWS)

    @pl.when(s == 0)
    def _():
        halo_ref[...] = jnp.zeros_like(halo_ref)

    for sub in range(x_ref.shape[0] // ts):
        rows = pl.ds(sub * ts, ts)
        x = x_ref[rows, :]
        xn = x * _rms_scale(x) * gmix_ref[...]
        ext = jnp.concatenate([halo_ref[...], xn], axis=0)
        halo_ref[...] = xn[ts - POOL_HALO:, :]
        pos = (s * x_ref.shape[0] + sub * ts) + lax.broadcasted_iota(jnp.int32, (ts, LANES), 0)
        for gi, w in enumerate(POOL_WINDOWS):
            acc = ext[:, gi * c:(gi + 1) * c]
            k = 1
            while k < w:
                acc = acc + pltpu.roll(acc, k, 0)
                k *= 2
            inv_cnt = 1.0 / jnp.minimum(pos + 1, w).astype(F32)
            inv_cnt = jnp.concatenate([inv_cnt] * (c // LANES), axis=1)
            pooled = acc[POOL_HALO:, :] * inv_cnt - xn[:, gi * c:(gi + 1) * c]
            y = jnp.dot(pooled.astype(BF16), pw_ref[gi], preferred_element_type=F32)
            h_ref[rows, gi * c:(gi + 1) * c] = (x[:, gi * c:(gi + 1) * c]
                                                + y * pscale_ref[:, gi * c:(gi + 1) * c])
        _ffn_norm_route(h_ref[rows, :], gffn_ref, wr_ref, br_ref, xr_ref, cls_ref, sub, n_grp, n_exp)


def _pool_route(x2, gmix, pw, pscale, gffn, wr, br, batch, n_grp, n_exp):
    t, d = x2.shape
    ts, step = TOKEN_TILE, ROUTE_STEP
    seq_steps = t // batch // step
    tok = lambda b, s: (b * seq_steps + s, 0)
    fixed2 = lambda b, s: (0, 0)
    return pl.pallas_call(
        functools.partial(_pool_route_kernel, n_grp=n_grp, n_exp=n_exp),
        grid=(batch, seq_steps),
        in_specs=[pl.BlockSpec((step, d), tok),
                  pl.BlockSpec((1, d), fixed2),
                  pl.BlockSpec(pw.shape, lambda b, s: (0, 0, 0)),
                  pl.BlockSpec((1, d), fixed2),
                  pl.BlockSpec((1, d), fixed2),
                  pl.BlockSpec((d, LANES), fixed2),
                  pl.BlockSpec((1, LANES), fixed2)],
        out_specs=(pl.BlockSpec((step, d), tok),
                   pl.BlockSpec((step, d // 2 + LANES), tok),
                   pl.BlockSpec((step // ts, SUBLANES, ts), lambda b, s: (b * seq_steps + s, 0, 0))),
        out_shape=_route_out(t, d, ts),
        scratch_shapes=[pltpu.VMEM((POOL_HALO, d), F32)],
        compiler_params=_params("arbitrary", "arbitrary"),
        name="pool_route",
    )(x2, gmix, pw, pscale, gffn, wr, br)


def _positions_kernel(cls_ref, pos_ref, tiles_ref, *, tm):
    n_t, _, ts = cls_ref.shape
    ci = lax.broadcasted_iota(jnp.int32, (CLASS_ROWS, ts), 0).astype(F32)

    def onehot(i):
        return (cls_ref[i, 0:1, :] == ci).astype(F32)

    acc = lax.fori_loop(0, n_t, lambda i, a: a + onehot(i), jnp.zeros((CLASS_ROWS, ts), F32))
    counts = jnp.sum(acc, axis=1, keepdims=True)
    tiles = jnp.floor((counts + (tm - 1)) * (1.0 / tm))
    tiles_b = jnp.broadcast_to(tiles, (CLASS_ROWS, LANES))
    tiles_ref[...] = tiles_b
    rr = lax.broadcasted_iota(jnp.int32, (CLASS_ROWS, CLASS_ROWS), 0)
    cc = lax.broadcasted_iota(jnp.int32, (CLASS_ROWS, CLASS_ROWS), 1)
    before = (cc < rr).astype(BF16)
    start = jnp.dot(before, tiles_b.astype(BF16), preferred_element_type=F32)[:, 0:1] * tm
    jj = lax.broadcasted_iota(jnp.int32, (ts, ts), 0)
    ss = lax.broadcasted_iota(jnp.int32, (ts, ts), 1)
    earlier = (jj < ss).astype(BF16)

    def body(i, run):
        oh = onehot(i)
        rank = jnp.dot(oh.astype(BF16), earlier, preferred_element_type=F32)
        pos_ref[i] = jnp.sum(oh * (run + rank), axis=0, keepdims=True).astype(jnp.int32)
        return run + jnp.sum(oh, axis=1, keepdims=True)

    lax.fori_loop(0, n_t, body, start)


def _positions(cls3, tm):
    n_t, _, ts = cls3.shape
    return pl.pallas_call(
        functools.partial(_positions_kernel, tm=tm),
        out_shape=(jax.ShapeDtypeStruct((n_t, 1, ts), jnp.int32),
                   jax.ShapeDtypeStruct((CLASS_ROWS, LANES), F32)),
        compiler_params=pltpu.CompilerParams(vmem_limit_bytes=VMEM_LIMIT),
        name="positions",
    )(cls3)


def _start_rows(n, make):
    for r in range(n):
        make(r).start()


def _dispatch_kernel(tail_ref, pos_ref, xr_ref, xs_hbm, rows_ref, zero_ref, sems, zsem, *, tm, n_tiles):
    i = pl.program_id(0)
    n_steps = pl.num_programs(0)
    n_sub, ts = rows_ref.shape[0], rows_ref.shape[1]
    n_cls = tail_ref.shape[0] - 1

    @pl.when(i == 0)
    def _():
        zero_ref[...] = jnp.zeros_like(zero_ref)
        n_act = tail_ref[n_cls]

        def zero_tile(start):
            return pltpu.make_async_copy(zero_ref, xs_hbm.at[pl.ds(pl.multiple_of(start, tm), tm)], zsem)

        todo = [(tail_ref[c] >= 0, jnp.maximum(tail_ref[c], 0)) for c in range(n_cls)]
        todo += [(n_act + c < n_tiles, jnp.minimum(n_act + c, n_tiles - 1) * tm) for c in range(n_cls)]
        for needed, start in todo:
            @pl.when(needed)
            def _():
                zero_tile(start).start()
        for needed, start in todo:
            @pl.when(needed)
            def _():
                zero_tile(start).wait()

    def wait_rows(s):
        pltpu.make_async_copy(rows_ref.at[s], xs_hbm.at[pl.ds(0, ts)], sems.at[s]).wait()

    for s in range(n_sub):
        @pl.when(i > 0)
        def _():
            wait_rows(s)

        rows_ref[s] = xr_ref[pl.ds(s * ts, ts), :]
        _start_rows(ts, lambda r: pltpu.make_async_copy(
            rows_ref.at[s, pl.ds(r, 1)], xs_hbm.at[pl.ds(pos_ref[0, 0, s * ts + r], 1)], sems.at[s]))

    @pl.when(i == n_steps - 1)
    def _():
        for s in range(n_sub):
            wait_rows(s)


def _dispatch(tail, pos3, xr, n_tiles, tm):
    t, dw = xr.shape
    ts = pos3.shape[2]
    n_sub = PERMUTE_SUBTILES
    pos_step = pos3.reshape(-1, 1, n_sub * ts)
    return pl.pallas_call(
        functools.partial(_dispatch_kernel, tm=tm, n_tiles=n_tiles),
        grid_spec=pltpu.PrefetchScalarGridSpec(
            num_scalar_prefetch=1,
            grid=(pos_step.shape[0],),
            in_specs=[pl.BlockSpec((1, 1, n_sub * ts), lambda i, tail: (i, 0, 0), memory_space=pltpu.SMEM),
                      pl.BlockSpec((n_sub * ts, dw), lambda i, tail: (i, 0))],
            out_specs=pl.BlockSpec(memory_space=pl.ANY),
            scratch_shapes=[pltpu.VMEM((n_sub, ts, dw), WORD),
                            pltpu.VMEM((tm, dw), WORD),
                            pltpu.SemaphoreType.DMA((n_sub,)),
                            pltpu.SemaphoreType.DMA(())]),
        out_shape=jax.ShapeDtypeStruct((n_tiles * tm, dw), WORD),
        compiler_params=_params("arbitrary"),
        name="dispatch",
    )(tail, pos_step, xr)


def _moe_kernel(elo_ref, ehi_ref, nact_ref, xs_ref, wg_lo, wg_hi, wu_lo, wu_hi, wd_lo, wd_hi, ys_ref):
    half = ys_ref.shape[1]
    active = pl.program_id(0) < nact_ref[0]

    @pl.when(active)
    def _():
        x_hi, x_lo = _unpack_pairs(xs_ref[:, :half])
        x = jnp.concatenate([x_hi.astype(BF16), x_lo.astype(BF16)], axis=1)

        def expert(wg, wu, lane):
            g = jnp.dot(x, wg[0], preferred_element_type=F32)
            u = jnp.dot(x, wu[0], preferred_element_type=F32)
            gate = lax.bitcast_convert_type(xs_ref[:, half + lane:half + lane + 1], F32)
            return (g * (1.0 / (1.0 + jnp.exp(-g))) * u * gate).astype(BF16)

        y = (jnp.dot(expert(wg_lo, wu_lo, 1), wd_lo[0], preferred_element_type=F32)
             + jnp.dot(expert(wg_hi, wu_hi, 2), wd_hi[0], preferred_element_type=F32))
        ys_ref[...] = _pack_pairs(y[:, :half], y[:, half:])

    @pl.when(jnp.logical_not(active))
    def _():
        ys_ref[...] = jnp.zeros_like(ys_ref)


def _moe(e_lo, e_hi, n_act, xs, wg, wu, wd, tm):
    n_rows, dw = xs.shape
    half = dw - LANES
    d = 2 * half
    f = wg.shape[2]
    tile = lambda i, lo, hi, na: (jnp.minimum(i, na[0] - 1), 0)
    w_lo = lambda i, lo, hi, na: (lo[i], 0, 0)
    w_hi = lambda i, lo, hi, na: (hi[i], 0, 0)
    return pl.pallas_call(
        _moe_kernel,
        grid_spec=pltpu.PrefetchScalarGridSpec(
            num_scalar_prefetch=3,
            grid=(n_rows // tm,),
            in_specs=[pl.BlockSpec((tm, dw), tile),
                      pl.BlockSpec((1, d, f), w_lo), pl.BlockSpec((1, d, f), w_hi),
                      pl.BlockSpec((1, d, f), w_lo), pl.BlockSpec((1, d, f), w_hi),
                      pl.BlockSpec((1, f, d), w_lo), pl.BlockSpec((1, f, d), w_hi)],
            out_specs=pl.BlockSpec((tm, half), lambda i, lo, hi, na: (i, 0))),
        out_shape=jax.ShapeDtypeStruct((n_rows, half), WORD),
        compiler_params=_params("arbitrary"),
        name="moe",
    )(e_lo, e_hi, n_act, xs, wg, wg, wu, wu, wd, wd)


def _combine_kernel(pos_ref, posn_ref, h_ref, ys_hbm, out_ref, rows_ref, sems):
    i = pl.program_id(0)
    n_steps = pl.num_programs(0)
    n_sub, ts = rows_ref.shape[0], rows_ref.shape[1]

    def gather(p_ref, s):
        _start_rows(ts, lambda r: pltpu.make_async_copy(
            ys_hbm.at[pl.ds(p_ref[0, 0, s * ts + r], 1)], rows_ref.at[s, pl.ds(r, 1)], sems.at[s]))

    @pl.when(i == 0)
    def _():
        for s in range(n_sub):
            gather(pos_ref, s)

    for s in range(n_sub):
        rows = pl.ds(s * ts, ts)
        pltpu.make_async_copy(ys_hbm.at[pl.ds(0, ts)], rows_ref.at[s], sems.at[s]).wait()
        first, second = _unpack_pairs(rows_ref[s])
        out_ref[rows, :] = h_ref[rows, :] + jnp.concatenate([first, second], axis=1)

        @pl.when(i + 1 < n_steps)
        def _():
            gather(posn_ref, s)


def _combine(pos3, h, ys):
    t, d = h.shape
    ts = pos3.shape[2]
    n_sub = PERMUTE_SUBTILES
    pos_step = pos3.reshape(-1, 1, n_sub * ts)
    n_steps = pos_step.shape[0]
    tc = n_sub * ts
    return pl.pallas_call(
        _combine_kernel,
        grid=(n_steps,),
        in_specs=[pl.BlockSpec((1, 1, tc), lambda i: (i, 0, 0), memory_space=pltpu.SMEM),
                  pl.BlockSpec((1, 1, tc), lambda i: (jnp.minimum(i + 1, n_steps - 1), 0, 0),
                               memory_space=pltpu.SMEM),
                  pl.BlockSpec((tc, d), lambda i: (i, 0)),
                  pl.BlockSpec(memory_space=pl.ANY)],
        out_specs=pl.BlockSpec((tc, d), lambda i: (i, 0)),
        out_shape=jax.ShapeDtypeStruct((t, d), F32),
        scratch_shapes=[pltpu.VMEM((n_sub, ts, d // 2), WORD), pltpu.SemaphoreType.DMA((n_sub,))],
        compiler_params=_params("arbitrary"),
        name="combine",
    )(pos_step, pos_step, h, ys)


def _kvq_kernel(h_ref, gkv_ref, gq_ref, w_ref, gain_ref, out_ref, xkv_ref, xq_ref, *, d):
    n = pl.program_id(1)
    bn = out_ref.shape[1]
    k_blocks, kv_blocks = d // bn, 2 * d // bn

    @pl.when(n == 0)
    def _():
        h = h_ref[...]
        xhat = h * _rms_scale(h)
        xkv_ref[...] = (xhat * gkv_ref[...]).astype(BF16)
        xq_ref[...] = (xhat * gq_ref[...]).astype(BF16)

    def project(x_ref, head_norm):
        x = x_ref[...]
        for c0 in range(0, bn, MXU_COLS):
            acc = jnp.dot(x, w_ref[:, c0:c0 + MXU_COLS], preferred_element_type=F32)
            if head_norm:
                for hh in range(MXU_COLS // HEAD_DIM):
                    cols = slice(c0 + hh * HEAD_DIM, c0 + (hh + 1) * HEAD_DIM)
                    blk = acc[:, hh * HEAD_DIM:(hh + 1) * HEAD_DIM]
                    out_ref[:, cols] = (blk * _rms_scale(blk) * gain_ref[:, cols]).astype(BF16)
            else:
                out_ref[:, c0:c0 + MXU_COLS] = acc.astype(BF16)

    @pl.when(n < k_blocks)
    def _():
        project(xkv_ref, True)

    @pl.when((n >= k_blocks) & (n < kv_blocks))
    def _():
        project(xkv_ref, False)

    @pl.when(n >= kv_blocks)
    def _():
        project(xq_ref, True)


def _kvq(h, gkv, gq, w_all, gain_all):
    t, d = h.shape
    n_out = w_all.shape[1]
    tm, bn = PROJ_TILE, PROJ_COLS
    return pl.pallas_call(
        functools.partial(_kvq_kernel, d=d),
        grid=(t // tm, n_out // bn),
        in_specs=[pl.BlockSpec((tm, d), lambda i, n: (i, 0)),
                  pl.BlockSpec((1, d), lambda i, n: (0, 0)),
                  pl.BlockSpec((1, d), lambda i, n: (0, 0)),
                  pl.BlockSpec((d, bn), lambda i, n: (0, n)),
                  pl.BlockSpec((1, bn), lambda i, n: (0, n))],
        out_specs=pl.BlockSpec((tm, bn), lambda i, n: (i, n)),
        out_shape=jax.ShapeDtypeStruct((t, n_out), BF16),
        scratch_shapes=[pltpu.VMEM((tm, d), BF16), pltpu.VMEM((tm, d), BF16)],
        compiler_params=_params("arbitrary", "arbitrary"),
        name="kvq",
    )(h, gkv, gq, w_all, gain_all)


def _attn_kernel(q_ref, k_ref, v_ref, o_ref, acc_ref, carry_ref, logit_ref, mass_ref):
    qi = pl.program_id(2)
    tq = q_ref.shape[0]
    tk = ATTN_KEYS
    n_heads = q_ref.shape[1] // HEAD_DIM
    n_before = (qi * tq) // tk
    jj = lax.broadcasted_iota(jnp.int32, (tk, tk), 0)
    ss = lax.broadcasted_iota(jnp.int32, (tk, tk), 1)
    later = (jj > ss).astype(BF16)
    rows = qi * tq + lax.broadcasted_iota(jnp.int32, (tq, tk), 0)
    cols = n_before * tk + lax.broadcasted_iota(jnp.int32, (tq, tk), 1)
    causal = cols < rows
    sign_bit = jnp.uint32(0x80000000)

    def keys(j):
        return pl.ds(pl.multiple_of(j * tk, tk), tk)

    def score(j, diagonal):
        partial, softplus, first = [], [], []
        for g in range(n_heads):
            hd = slice(g * HEAD_DIM, (g + 1) * HEAD_DIM)
            z = lax.dot_general(q_ref[:, hd], k_ref[keys(j), hd], (((1,), (1,)), ((), ())),
                                preferred_element_type=F32)
            if diagonal:
                z = jnp.where(causal, z, -jnp.inf)
            neg_abs = lax.bitcast_convert_type(lax.bitcast_convert_type(z, WORD) | sign_bit, F32)
            sp = jnp.maximum(z, 0.0) + jnp.log2(1.0 + jnp.exp2(neg_abs))
            partial.append(z - sp)
            softplus.append(sp.astype(BF16))
            first.append(sp[:, 0:1])
        after = jnp.dot(jnp.concatenate(softplus, axis=0), later, preferred_element_type=F32)
        for g in range(n_heads):
            after_g = after[g * tq:(g + 1) * tq, :]
            logit_ref[g] = partial[g] - after_g
            mass_ref[g] = jnp.broadcast_to(first[g] + after_g[:, 0:1], (tq, LANES))

    def accumulate(g, j):
        hd = slice(g * HEAD_DIM, (g + 1) * HEAD_DIM)
        c = carry_ref[g]
        a = jnp.exp2(logit_ref[g] - jnp.concatenate([c] * (tk // LANES), axis=1))
        acc_ref[g] += jnp.dot(a.astype(BF16), v_ref[keys(j), hd], preferred_element_type=F32)
        carry_ref[g] = c + mass_ref[g]

    def alive():
        return jnp.min(carry_ref[...] + mass_ref[...]) < ATTN_DEAD_MASS

    acc_ref[...] = jnp.zeros_like(acc_ref)
    carry_ref[...] = jnp.zeros_like(carry_ref)
    score(n_before, True)

    def body(state):
        i, _ = state
        j = n_before - 1 - i
        for g in range(n_heads):
            accumulate(g, j + 1)
        score(j, False)
        return i + 1, alive()

    start_alive = jnp.bool_(True) if tq == tk else alive()
    n_done, _ = lax.while_loop(lambda state: (state[0] < n_before) & state[1], body,
                               (jnp.int32(0), start_alive))
    for g in range(n_heads):
        accumulate(g, n_before - n_done)
        o_ref[:, g * HEAD_DIM:(g + 1) * HEAD_DIM] = acc_ref[g].astype(BF16)


def _attention(kvq, batch, d):
    t = kvq.shape[0]
    seq = t // batch
    gw = ATTN_HEADS * HEAD_DIM
    n_hg = d // gw
    tq = ATTN_TILE
    nq = seq // tq
    return pl.pallas_call(
        _attn_kernel,
        grid=(batch, n_hg, nq),
        in_specs=[pl.BlockSpec((tq, gw), lambda b, h, i: (b * nq + i, 2 * n_hg + h)),
                  pl.BlockSpec((seq, gw), lambda b, h, i: (b, h)),
                  pl.BlockSpec((seq, gw), lambda b, h, i: (b, n_hg + h))],
        out_specs=pl.BlockSpec((tq, gw), lambda b, h, i: (b * nq + i, h)),
        out_shape=jax.ShapeDtypeStruct((t, d), BF16),
        scratch_shapes=[pltpu.VMEM((ATTN_HEADS, tq, HEAD_DIM), F32),
                        pltpu.VMEM((ATTN_HEADS, tq, LANES), F32),
                        pltpu.VMEM((ATTN_HEADS, tq, ATTN_KEYS), F32),
                        pltpu.VMEM((ATTN_HEADS, tq, LANES), F32)],
        compiler_params=_params("arbitrary", "arbitrary", "arbitrary"),
        name="attention",
    )(kvq, kvq, kvq)


def _oproj_route_kernel(o_ref, wo_ref, h_ref, gffn_ref, wr_ref, br_ref,
                        hout_ref, xr_ref, cls_ref, *, n_grp, n_exp):
    ts = TOKEN_TILE
    for sub in range(o_ref.shape[0] // ts):
        rows = pl.ds(sub * ts, ts)
        h = h_ref[rows, :] + jnp.dot(o_ref[rows, :], wo_ref[...], preferred_element_type=F32)
        hout_ref[rows, :] = h
        _ffn_norm_route(h, gffn_ref, wr_ref, br_ref, xr_ref, cls_ref, sub, n_grp, n_exp)


def _oproj_route(o, wo, h, gffn, wr, br, n_grp, n_exp):
    t, d = h.shape
    ts, step = TOKEN_TILE, ROUTE_STEP
    tok = lambda i: (i, 0)
    fixed2 = lambda i: (0, 0)
    return pl.pallas_call(
        functools.partial(_oproj_route_kernel, n_grp=n_grp, n_exp=n_exp),
        grid=(t // step,),
        in_specs=[pl.BlockSpec((step, d), tok),
                  pl.BlockSpec((d, d), fixed2),
                  pl.BlockSpec((step, d), tok),
                  pl.BlockSpec((1, d), fixed2),
                  pl.BlockSpec((d, LANES), fixed2),
                  pl.BlockSpec((1, LANES), fixed2)],
        out_specs=(pl.BlockSpec((step, d), tok),
                   pl.BlockSpec((step, d // 2 + LANES), tok),
                   pl.BlockSpec((step // ts, SUBLANES, ts), lambda i: (i, 0, 0))),
        out_shape=_route_out(t, d, ts),
        compiler_params=_params("arbitrary"),
        name="oproj_route",
    )(o, wo, h, gffn, wr, br)


def _router_weights(w_grp, b_grp, w_rt, b_rt):
    d, n_grp = w_grp.shape
    n_exp = w_rt.shape[2]
    w = jnp.concatenate([w_grp, jnp.transpose(w_rt, (1, 0, 2)).reshape(d, n_grp * n_exp)], axis=1)
    b = jnp.concatenate([b_grp, b_rt.reshape(-1)])
    pad = LANES - w.shape[1]
    return (jnp.pad(w, ((0, 0), (0, pad))).astype(BF16), jnp.pad(b, (0, pad)).reshape(1, LANES))


def _pair_table(n_grp, n_exp):
    lo, hi = [], []
    for g in range(n_grp):
        for a in range(n_exp):
            for b in range(a + 1, n_exp):
                lo.append(g * n_exp + a)
                hi.append(g * n_exp + b)
    return jnp.array(lo, jnp.int32), jnp.array(hi, jnp.int32)


def _moe_layer(h, xr, cls3, wg, wu, wd, first_expert, n_grp, n_exp):
    t, d = h.shape
    tm = MOE_TILE
    cls_lo, cls_hi = _pair_table(n_grp, n_exp)
    n_cls = cls_lo.shape[0]
    n_tiles = t // tm + n_cls
    assert n_cls <= CLASS_ROWS and n_tiles < 256
    pos3, tiles = _positions(cls3, tm)
    tiles_c = tiles[:n_cls, 0].astype(jnp.int32)
    ends = jnp.cumsum(tiles_c)
    n_act = ends[-1]
    tail = jnp.where(tiles_c > 0, (ends - 1) * tm, -1).astype(jnp.int32)
    tile_ids = jnp.minimum(jnp.arange(n_tiles, dtype=jnp.int32), n_act - 1)
    tile_cls = jnp.sum(tile_ids[:, None] >= ends[None, :], axis=1)
    xs = _dispatch(jnp.concatenate([tail, n_act.reshape(1)]), pos3, xr, n_tiles, tm)
    ys = _moe(first_expert + cls_lo[tile_cls], first_expert + cls_hi[tile_cls], n_act.reshape(1),
              xs, wg, wu, wd, tm)
    return _combine(pos3, h, ys)


def kernel(x, norm_mix, norm_ffn, pool_w, pool_scale, kv_norm, w_kv, k_norm, w_q, q_norm, w_o,
           moe_w_grp, moe_b_grp, moe_w_rt, moe_b_rt, moe_w_gate, moe_w_up, moe_w_down):
    batch, seq, d = x.shape
    t = batch * seq
    n_grp, n_exp = moe_w_rt.shape[1], moe_w_rt.shape[3]
    n_heads = d // HEAD_DIM
    assert len(POOL_WINDOWS) == pool_w.shape[1] and seq % ROUTE_STEP == 0 and seq % ATTN_TILE == 0
    assert t % PROJ_TILE == 0 and d % PROJ_COLS == 0 and n_grp + n_grp * n_exp <= LANES
    assert n_heads % ATTN_HEADS == 0 and seq % ATTN_KEYS == 0 and ATTN_KEYS % ATTN_TILE == 0
    row = lambda v: v.reshape(1, -1).astype(F32)

    n_all = moe_w_gate.shape[1]
    f = moe_w_gate.shape[3]
    wg = _cast_bf16([moe_w_gate.reshape(-1, f)]).reshape(-1, d, f)
    wu = _cast_bf16([moe_w_up.reshape(-1, f)]).reshape(-1, d, f)
    wd = _cast_bf16([moe_w_down.reshape(-1, d)]).reshape(-1, f, d)

    wr0, br0 = _router_weights(moe_w_grp[0], moe_b_grp[0], moe_w_rt[0], moe_b_rt[0])
    h1, xr, cls3 = _pool_route(x.reshape(t, d), row(norm_mix[0]), pool_w[0].astype(BF16),
                               row(pool_scale[0]), row(norm_ffn[0]), wr0, br0, batch, n_grp, n_exp)
    h2 = _moe_layer(h1, xr, cls3, wg, wu, wd, 0, n_grp, n_exp)

    w_all = _cast_bf16([w_kv, w_q[0]])
    q_gain = q_norm[0] * (HEAD_DIM ** -0.5 * LOG2E)
    gain_all = jnp.concatenate([jnp.tile(k_norm, n_heads), jnp.ones((d,), F32), jnp.tile(q_gain, n_heads)])
    kvq = _kvq(h2, row(kv_norm), row(norm_mix[1]), w_all, row(gain_all))
    o = _attention(kvq, batch, d)

    wr1, br1 = _router_weights(moe_w_grp[1], moe_b_grp[1], moe_w_rt[1], moe_b_rt[1])
    h3, xr, cls3 = _oproj_route(o, _cast_bf16([w_o[0]]), h2, row(norm_ffn[1]), wr1, br1, n_grp, n_exp)
    h4 = _moe_layer(h3, xr, cls3, wg, wu, wd, n_all, n_grp, n_exp)
    return h4.reshape(batch, seq, d)
```

```python
import functools

import jax
import jax.numpy as jnp
from jax import lax
from jax.experimental import pallas as pl
from jax.experimental.pallas import tpu as pltpu

F32 = jnp.float32
BF16 = jnp.bfloat16
WORD = jnp.uint32

EPS = 1e-6
POOL_WINDOWS = (2, 4, 8, 16)
POOL_HALO = 16
HEAD_DIM = 128
LANES = 128
SUBLANES = 8
MXU_COLS = 256
LOG2E = 1.4426950408889634

TOKEN_TILE = 256
ROUTE_STEP = 512
MOE_TILE = 256
PROJ_TILE = 1024
PROJ_COLS = 1024
ATTN_TILE = 256
ATTN_KEYS = 256
ATTN_DEAD_MASS = 160.0
ATTN_HEADS = 8
PERMUTE_SUBTILES = 2
CLASS_ROWS = 32
CAST_BLOCK_ELEMS = 2 * 1024 * 1024
VMEM_LIMIT = 56 * 1024 * 1024


def _params(*sem):
    return pltpu.CompilerParams(dimension_semantics=sem, vmem_limit_bytes=VMEM_LIMIT)


def _rms_scale(x):
    return lax.rsqrt(jnp.mean(x * x, axis=-1, keepdims=True) + EPS)


def _pack_pairs(first, second):
    return pltpu.pack_elementwise([first, second], packed_dtype=BF16)


def _unpack_pairs(words):
    return tuple(pltpu.unpack_elementwise(words, index=i, packed_dtype=BF16, unpacked_dtype=F32)
                 for i in range(2))


def _cast_kernel(*refs):
    out_ref = refs[-1]
    c0 = 0
    for ref in refs[:-1]:
        out_ref[:, c0:c0 + ref.shape[1]] = ref[...].astype(BF16)
        c0 += ref.shape[1]


def _cast_bf16(parts):
    rows = parts[0].shape[0]
    cols = sum(p.shape[1] for p in parts)
    rb = rows
    while rb * cols > CAST_BLOCK_ELEMS and rb % 32 == 0:
        rb //= 2
    return pl.pallas_call(
        _cast_kernel,
        grid=(rows // rb,),
        in_specs=[pl.BlockSpec((rb, p.shape[1]), lambda i: (i, 0)) for p in parts],
        out_specs=pl.BlockSpec((rb, cols), lambda i: (i, 0)),
        out_shape=jax.ShapeDtypeStruct((rows, cols), BF16),
        compiler_params=_params("arbitrary"),
        name="cast_bf16",
    )(*parts)


def _first_argmax(vals):
    best, idx = vals[0], jnp.zeros(vals[0].shape, jnp.int32)
    for i in range(1, len(vals)):
        upd = vals[i] > best
        best = jnp.where(upd, vals[i], best)
        idx = jnp.where(upd, i, idx)
    return best, idx


def _ffn_norm_route(h, gffn_ref, wr_ref, br_ref, xr_ref, cls_ref, sub, n_grp, n_exp):
    ts, d = h.shape
    half = d // 2
    rows = pl.ds(sub * ts, ts)
    xn = h * _rms_scale(h) * gffn_ref[...]
    xr_ref[rows, :half] = _pack_pairs(xn[:, :half], xn[:, half:])
    logits = jnp.dot(xn.astype(BF16), wr_ref[...], preferred_element_type=F32) + br_ref[...]
    lt = logits.T
    row = lambda i: lt[i:i + 1, :]
    g_best, g_idx = _first_argmax([row(g) for g in range(n_grp)])
    denom = sum(jnp.exp(row(g) - g_best) for g in range(n_grp))
    g_w = 1.0 / denom
    sel = []
    for e in range(n_exp):
        v = row(n_grp + e)
        for g in range(1, n_grp):
            v = jnp.where(g_idx == g, row(n_grp + g * n_exp + e), v)
        sel.append(v)
    v1, i1 = _first_argmax(sel)
    v2 = jnp.full_like(v1, -jnp.inf)
    i2 = jnp.zeros_like(i1)
    for e in range(n_exp):
        upd = (i1 != e) & (sel[e] > v2)
        v2 = jnp.where(upd, sel[e], v2)
        i2 = jnp.where(upd, e, i2)
    t = jnp.exp(v2 - v1)
    w1 = g_w / (1.0 + t)
    w2 = w1 * t
    first_low = i1 < i2
    lo = jnp.where(first_low, i1, i2)
    hi = jnp.where(first_low, i2, i1)
    w_lo = jnp.where(first_low, w1, w2)
    w_hi = jnp.where(first_low, w2, w1)
    n_pairs = n_exp * (n_exp - 1) // 2
    pair = ((lo * (2 * n_exp - 1 - lo)) >> 1) + (hi - lo - 1)
    cls = (g_idx * n_pairs + pair).astype(F32)
    r = lax.broadcasted_iota(jnp.int32, (LANES, ts), 0)
    packed = jnp.where(r == 0, cls, jnp.where(r == 1, w_lo, jnp.where(r == 2, w_hi, 0.0)))
    cls_ref[sub] = packed[:SUBLANES, :]
    xr_ref[rows, half:] = lax.bitcast_convert_type(packed.T, xr_ref.dtype)


def _route_out(t, d, ts):
    return (jax.ShapeDtypeStruct((t, d), F32),
            jax.ShapeDtypeStruct((t, d // 2 + LANES), WORD),
            jax.ShapeDtypeStruct((t // ts, SUBLANES, ts), F32))


def _pool_route_kernel(x_ref, gmix_ref, pw_ref, pscale_ref, gffn_ref, wr_ref, br_ref,
                       h_ref, xr_ref, cls_ref, halo_ref, *, n_grp, n_exp):
    s = pl.program_id(1)
    d = x_ref.shape[1]
    ts = TOKEN_TILE
    c = d // len(POOL_WINDOWS)

    @pl.when(s == 0)
    def _():
        halo_ref[...] = jnp.zeros_like(halo_ref)

    for sub in range(x_ref.shape[0] // ts):
        rows = pl.ds(sub * ts, ts)
        x = x_ref[rows, :]
        xn = x * _rms_scale(x) * gmix_ref[...]
        ext = jnp.concatenate([halo_ref[...], xn], axis=0)
        halo_ref[...] = xn[ts - POOL_HALO:, :]
        pos = (s * x_ref.shape[0] + sub * ts) + lax.broadcasted_iota(jnp.int32, (ts, LANES), 0)
        for gi, w in enumerate(POOL_WINDOWS):
            acc = ext[:, gi * c:(gi + 1) * c]
            k = 1
            while k < w:
                acc = acc + pltpu.roll(acc, k, 0)
                k *= 2
            inv_cnt = 1.0 / jnp.minimum(pos + 1, w).astype(F32)
            inv_cnt = jnp.concatenate([inv_cnt] * (c // LANES), axis=1)
            pooled = acc[POOL_HALO:, :] * inv_cnt - xn[:, gi * c:(gi + 1) * c]
            y = jnp.dot(pooled.astype(BF16), pw_ref[gi], preferred_element_type=F32)
            h_ref[rows, gi * c:(gi + 1) * c] = (x[:, gi * c:(gi + 1) * c]
                                                + y * pscale_ref[:, gi * c:(gi + 1) * c])
        _ffn_norm_route(h_ref[rows, :], gffn_ref, wr_ref, br_ref, xr_ref, cls_ref, sub, n_grp, n_exp)


def _pool_route(x2, gmix, pw, pscale, gffn, wr, br, batch, n_grp, n_exp):
    t, d = x2.shape
    ts, step = TOKEN_TILE, ROUTE_STEP
    seq_steps = t // batch // step
    tok = lambda b, s: (b * seq_steps + s, 0)
    fixed2 = lambda b, s: (0, 0)
    return pl.pallas_call(
        functools.partial(_pool_route_kernel, n_grp=n_grp, n_exp=n_exp),
        grid=(batch, seq_steps),
        in_specs=[pl.BlockSpec((step, d), tok),
                  pl.BlockSpec((1, d), fixed2),
                  pl.BlockSpec(pw.shape, lambda b, s: (0, 0, 0)),
                  pl.BlockSpec((1, d), fixed2),
                  pl.BlockSpec((1, d), fixed2),
                  pl.BlockSpec((d, LANES), fixed2),
                  pl.BlockSpec((1, LANES), fixed2)],
        out_specs=(pl.BlockSpec((step, d), tok),
                   pl.BlockSpec((step, d // 2 + LANES), tok),
                   pl.BlockSpec((step // ts, SUBLANES, ts), lambda b, s: (b * seq_steps + s, 0, 0))),
        out_shape=_route_out(t, d, ts),
        scratch_shapes=[pltpu.VMEM((POOL_HALO, d), F32)],
        compiler_params=_params("arbitrary", "arbitrary"),
        name="pool_route",
    )(x2, gmix, pw, pscale, gffn, wr, br)


def _positions_kernel(cls_ref, pos_ref, tiles_ref, *, tm):
    n_t, _, ts = cls_ref.shape
    ci = lax.broadcasted_iota(jnp.int32, (CLASS_ROWS, ts), 0).astype(F32)

    def onehot(i):
        return (cls_ref[i, 0:1, :] == ci).astype(F32)

    acc = lax.fori_loop(0, n_t, lambda i, a: a + onehot(i), jnp.zeros((CLASS_ROWS, ts), F32))
    counts = jnp.sum(acc, axis=1, keepdims=True)
    tiles = jnp.floor((counts + (tm - 1)) * (1.0 / tm))
    tiles_b = jnp.broadcast_to(tiles, (CLASS_ROWS, LANES))
    tiles_ref[...] = tiles_b
    rr = lax.broadcasted_iota(jnp.int32, (CLASS_ROWS, CLASS_ROWS), 0)
    cc = lax.broadcasted_iota(jnp.int32, (CLASS_ROWS, CLASS_ROWS), 1)
    before = (cc < rr).astype(BF16)
    start = jnp.dot(before, tiles_b.astype(BF16), preferred_element_type=F32)[:, 0:1] * tm
    jj = lax.broadcasted_iota(jnp.int32, (ts, ts), 0)
    ss = lax.broadcasted_iota(jnp.int32, (ts, ts), 1)
    earlier = (jj < ss).astype(BF16)

    def body(i, run):
        oh = onehot(i)
        rank = jnp.dot(oh.astype(BF16), earlier, preferred_element_type=F32)
        pos_ref[i] = jnp.sum(oh * (run + rank), axis=0, keepdims=True).astype(jnp.int32)
        return run + jnp.sum(oh, axis=1, keepdims=True)

    lax.fori_loop(0, n_t, body, start)


def _positions(cls3, tm):
    n_t, _, ts = cls3.shape
    return pl.pallas_call(
        functools.partial(_positions_kernel, tm=tm),
        out_shape=(jax.ShapeDtypeStruct((n_t, 1, ts), jnp.int32),
                   jax.ShapeDtypeStruct((CLASS_ROWS, LANES), F32)),
        compiler_params=pltpu.CompilerParams(vmem_limit_bytes=VMEM_LIMIT),
        name="positions",
    )(cls3)


def _start_rows(n, make):
    for r in range(n):
        make(r).start(priority=r % 2)


def _dispatch_kernel(tail_ref, pos_ref, xr_ref, xs_hbm, rows_ref, zero_ref, sems, zsem, *, tm, n_tiles):
    i = pl.program_id(0)
    n_steps = pl.num_programs(0)
    n_sub, ts = rows_ref.shape[0], rows_ref.shape[1]
    n_cls = tail_ref.shape[0] - 1

    @pl.when(i == 0)
    def _():
        zero_ref[...] = jnp.zeros_like(zero_ref)
        n_act = tail_ref[n_cls]

        def zero_tile(start):
            return pltpu.make_async_copy(zero_ref, xs_hbm.at[pl.ds(pl.multiple_of(start, tm), tm)], zsem)

        todo = [(tail_ref[c] >= 0, jnp.maximum(tail_ref[c], 0)) for c in range(n_cls)]
        todo += [(n_act + c < n_tiles, jnp.minimum(n_act + c, n_tiles - 1) * tm) for c in range(n_cls)]
        for needed, start in todo:
            @pl.when(needed)
            def _():
                zero_tile(start).start()
        for needed, start in todo:
            @pl.when(needed)
            def _():
                zero_tile(start).wait()

    def wait_rows(s):
        pltpu.make_async_copy(rows_ref.at[s], xs_hbm.at[pl.ds(0, ts)], sems.at[s]).wait()

    for s in range(n_sub):
        @pl.when(i > 0)
        def _():
            wait_rows(s)

        rows_ref[s] = xr_ref[pl.ds(s * ts, ts), :]
        _start_rows(ts, lambda r: pltpu.make_async_copy(
            rows_ref.at[s, pl.ds(r, 1)], xs_hbm.at[pl.ds(pos_ref[0, 0, s * ts + r], 1)], sems.at[s]))

    @pl.when(i == n_steps - 1)
    def _():
        for s in range(n_sub):
            wait_rows(s)


def _dispatch(tail, pos3, xr, n_tiles, tm):
    t, dw = xr.shape
    ts = pos3.shape[2]
    n_sub = PERMUTE_SUBTILES
    pos_step = pos3.reshape(-1, 1, n_sub * ts)
    return pl.pallas_call(
        functools.partial(_dispatch_kernel, tm=tm, n_tiles=n_tiles),
        grid_spec=pltpu.PrefetchScalarGridSpec(
            num_scalar_prefetch=1,
            grid=(pos_step.shape[0],),
            in_specs=[pl.BlockSpec((1, 1, n_sub * ts), lambda i, tail: (i, 0, 0), memory_space=pltpu.SMEM),
                      pl.BlockSpec((n_sub * ts, dw), lambda i, tail: (i, 0))],
            out_specs=pl.BlockSpec(memory_space=pl.ANY),
            scratch_shapes=[pltpu.VMEM((n_sub, ts, dw), WORD),
                            pltpu.VMEM((tm, dw), WORD),
                            pltpu.SemaphoreType.DMA((n_sub,)),
                            pltpu.SemaphoreType.DMA(())]),
        out_shape=jax.ShapeDtypeStruct((n_tiles * tm, dw), WORD),
        compiler_params=_params("arbitrary"),
        name="dispatch",
    )(tail, pos_step, xr)


def _moe_kernel(elo_ref, ehi_ref, nact_ref, xs_ref, wg_lo, wg_hi, wu_lo, wu_hi, wd_lo, wd_hi, ys_ref):
    half = ys_ref.shape[1]
    active = pl.program_id(0) < nact_ref[0]

    @pl.when(active)
    def _():
        x_hi, x_lo = _unpack_pairs(xs_ref[:, :half])
        x = jnp.concatenate([x_hi.astype(BF16), x_lo.astype(BF16)], axis=1)

        def expert(wg, wu, lane):
            g = jnp.dot(x, wg[0], preferred_element_type=F32)
            u = jnp.dot(x, wu[0], preferred_element_type=F32)
            gate = lax.bitcast_convert_type(xs_ref[:, half + lane:half + lane + 1], F32)
            return (g * (1.0 / (1.0 + jnp.exp(-g))) * u * gate).astype(BF16)

        y = (jnp.dot(expert(wg_lo, wu_lo, 1), wd_lo[0], preferred_element_type=F32)
             + jnp.dot(expert(wg_hi, wu_hi, 2), wd_hi[0], preferred_element_type=F32))
        ys_ref[...] = _pack_pairs(y[:, :half], y[:, half:])

    @pl.when(jnp.logical_not(active))
    def _():
        ys_ref[...] = jnp.zeros_like(ys_ref)


def _moe(e_lo, e_hi, n_act, xs, wg, wu, wd, tm):
    n_rows, dw = xs.shape
    half = dw - LANES
    d = 2 * half
    f = wg.shape[2]
    tile = lambda i, lo, hi, na: (jnp.minimum(i, na[0] - 1), 0)
    w_lo = lambda i, lo, hi, na: (lo[i], 0, 0)
    w_hi = lambda i, lo, hi, na: (hi[i], 0, 0)
    return pl.pallas_call(
        _moe_kernel,
        grid_spec=pltpu.PrefetchScalarGridSpec(
            num_scalar_prefetch=3,
            grid=(n_rows // tm,),
            in_specs=[pl.BlockSpec((tm, dw), tile),
                      pl.BlockSpec((1, d, f), w_lo), pl.BlockSpec((1, d, f), w_hi),
                      pl.BlockSpec((1, d, f), w_lo), pl.BlockSpec((1, d, f), w_hi),
                      pl.BlockSpec((1, f, d), w_lo), pl.BlockSpec((1, f, d), w_hi)],
            out_specs=pl.BlockSpec((tm, half), lambda i, lo, hi, na: (i, 0))),
        out_shape=jax.ShapeDtypeStruct((n_rows, half), WORD),
        compiler_params=_params("arbitrary"),
        name="moe",
    )(e_lo, e_hi, n_act, xs, wg, wg, wu, wu, wd, wd)


def _combine_kernel(pos_ref, posn_ref, h_ref, ys_hbm, out_ref, rows_ref, sems):
    i = pl.program_id(0)
    n_steps = pl.num_programs(0)
    n_sub, ts = rows_ref.shape[0], rows_ref.shape[1]

    def gather(p_ref, s):
        _start_rows(ts, lambda r: pltpu.make_async_copy(
            ys_hbm.at[pl.ds(p_ref[0, 0, s * ts + r], 1)], rows_ref.at[s, pl.ds(r, 1)], sems.at[s]))

    @pl.when(i == 0)
    def _():
        for s in range(n_sub):
            gather(pos_ref, s)

    for s in range(n_sub):
        rows = pl.ds(s * ts, ts)
        pltpu.make_async_copy(ys_hbm.at[pl.ds(0, ts)], rows_ref.at[s], sems.at[s]).wait()
        first, second = _unpack_pairs(rows_ref[s])
        out_ref[rows, :] = h_ref[rows, :] + jnp.concatenate([first, second], axis=1)

        @pl.when(i + 1 < n_steps)
        def _():
            gather(posn_ref, s)


def _combine(pos3, h, ys):
    t, d = h.shape
    ts = pos3.shape[2]
    n_sub = PERMUTE_SUBTILES
    pos_step = pos3.reshape(-1, 1, n_sub * ts)
    n_steps = pos_step.shape[0]
    tc = n_sub * ts
    return pl.pallas_call(
        _combine_kernel,
        grid=(n_steps,),
        in_specs=[pl.BlockSpec((1, 1, tc), lambda i: (i, 0, 0), memory_space=pltpu.SMEM),
                  pl.BlockSpec((1, 1, tc), lambda i: (jnp.minimum(i + 1, n_steps - 1), 0, 0),
                               memory_space=pltpu.SMEM),
                  pl.BlockSpec((tc, d), lambda i: (i, 0)),
                  pl.BlockSpec(memory_space=pl.ANY)],
        out_specs=pl.BlockSpec((tc, d), lambda i: (i, 0)),
        out_shape=jax.ShapeDtypeStruct((t, d), F32),
        scratch_shapes=[pltpu.VMEM((n_sub, ts, d // 2), WORD), pltpu.SemaphoreType.DMA((n_sub,))],
        compiler_params=_params("arbitrary"),
        name="combine",
    )(pos_step, pos_step, h, ys)


def _kvq_kernel(h_ref, gkv_ref, gq_ref, w_ref, gain_ref, out_ref, xkv_ref, xq_ref, *, d):
    n = pl.program_id(1)
    bn = out_ref.shape[1]
    k_blocks, kv_blocks = d // bn, 2 * d // bn

    @pl.when(n == 0)
    def _():
        h = h_ref[...]
        xhat = h * _rms_scale(h)
        xkv_ref[...] = (xhat * gkv_ref[...]).astype(BF16)
        xq_ref[...] = (xhat * gq_ref[...]).astype(BF16)

    def project(x_ref, head_norm):
        x = x_ref[...]
        for c0 in range(0, bn, MXU_COLS):
            acc = jnp.dot(x, w_ref[:, c0:c0 + MXU_COLS], preferred_element_type=F32)
            if head_norm:
                for hh in range(MXU_COLS // HEAD_DIM):
                    cols = slice(c0 + hh * HEAD_DIM, c0 + (hh + 1) * HEAD_DIM)
                    blk = acc[:, hh * HEAD_DIM:(hh + 1) * HEAD_DIM]
                    out_ref[:, cols] = (blk * _rms_scale(blk) * gain_ref[:, cols]).astype(BF16)
            else:
                out_ref[:, c0:c0 + MXU_COLS] = acc.astype(BF16)

    @pl.when(n < k_blocks)
    def _():
        project(xkv_ref, True)

    @pl.when((n >= k_blocks) & (n < kv_blocks))
    def _():
        project(xkv_ref, False)

    @pl.when(n >= kv_blocks)
    def _():
        project(xq_ref, True)


def _kvq(h, gkv, gq, w_all, gain_all):
    t, d = h.shape
    n_out = w_all.shape[1]
    tm, bn = PROJ_TILE, PROJ_COLS
    return pl.pallas_call(
        functools.partial(_kvq_kernel, d=d),
        grid=(t // tm, n_out // bn),
        in_specs=[pl.BlockSpec((tm, d), lambda i, n: (i, 0)),
                  pl.BlockSpec((1, d), lambda i, n: (0, 0)),
                  pl.BlockSpec((1, d), lambda i, n: (0, 0)),
                  pl.BlockSpec((d, bn), lambda i, n: (0, n)),
                  pl.BlockSpec((1, bn), lambda i, n: (0, n))],
        out_specs=pl.BlockSpec((tm, bn), lambda i, n: (i, n)),
        out_shape=jax.ShapeDtypeStruct((t, n_out), BF16),
        scratch_shapes=[pltpu.VMEM((tm, d), BF16), pltpu.VMEM((tm, d), BF16)],
        compiler_params=_params("arbitrary", "arbitrary"),
        name="kvq",
    )(h, gkv, gq, w_all, gain_all)


def _attn_kernel(q_ref, k_ref, v_ref, o_ref, acc_ref, carry_ref, logit_ref, mass_ref):
    qi = pl.program_id(2)
    tq = q_ref.shape[0]
    tk = ATTN_KEYS
    n_heads = q_ref.shape[1] // HEAD_DIM
    n_before = (qi * tq) // tk
    jj = lax.broadcasted_iota(jnp.int32, (tk, tk), 0)
    ss = lax.broadcasted_iota(jnp.int32, (tk, tk), 1)
    later = (jj > ss).astype(BF16)
    rows = qi * tq + lax.broadcasted_iota(jnp.int32, (tq, tk), 0)
    cols = n_before * tk + lax.broadcasted_iota(jnp.int32, (tq, tk), 1)
    causal = cols < rows
    sign_bit = jnp.uint32(0x80000000)

    def keys(j):
        return pl.ds(pl.multiple_of(j * tk, tk), tk)

    def score(j, diagonal):
        partial, softplus, first = [], [], []
        for g in range(n_heads):
            hd = slice(g * HEAD_DIM, (g + 1) * HEAD_DIM)
            z = lax.dot_general(q_ref[:, hd], k_ref[keys(j), hd], (((1,), (1,)), ((), ())),
                                preferred_element_type=F32)
            if diagonal:
                z = jnp.where(causal, z, -jnp.inf)
            neg_abs = lax.bitcast_convert_type(lax.bitcast_convert_type(z, WORD) | sign_bit, F32)
            sp = jnp.maximum(z, 0.0) + jnp.log2(1.0 + jnp.exp2(neg_abs))
            partial.append(z - sp)
            softplus.append(sp.astype(BF16))
            first.append(sp[:, 0:1])
        after = jnp.dot(jnp.concatenate(softplus, axis=0), later, preferred_element_type=F32)
        for g in range(n_heads):
            after_g = after[g * tq:(g + 1) * tq, :]
            logit_ref[g] = partial[g] - after_g
            mass_ref[g] = jnp.broadcast_to(first[g] + after_g[:, 0:1], (tq, LANES))

    def accumulate(g, j):
        hd = slice(g * HEAD_DIM, (g + 1) * HEAD_DIM)
        c = carry_ref[g]
        a = jnp.exp2(logit_ref[g] - jnp.concatenate([c] * (tk // LANES), axis=1))
        acc_ref[g] += jnp.dot(a.astype(BF16), v_ref[keys(j), hd], preferred_element_type=F32)
        carry_ref[g] = c + mass_ref[g]

    def alive():
        return jnp.min(carry_ref[...] + mass_ref[...]) < ATTN_DEAD_MASS

    acc_ref[...] = jnp.zeros_like(acc_ref)
    carry_ref[...] = jnp.zeros_like(carry_ref)
    score(n_before, True)

    def body(state):
        i, _ = state
        j = n_before - 1 - i
        for g in range(n_heads):
            accumulate(g, j + 1)
        score(j, False)
        return i + 1, alive()

    start_alive = jnp.bool_(True) if tq == tk else alive()
    n_done, _ = lax.while_loop(lambda state: (state[0] < n_before) & state[1], body,
                               (jnp.int32(0), start_alive))
    for g in range(n_heads):
        accumulate(g, n_before - n_done)
        o_ref[:, g * HEAD_DIM:(g + 1) * HEAD_DIM] = acc_ref[g].astype(BF16)


def _attention(kvq, batch, d):
    t = kvq.shape[0]
    seq = t // batch
    gw = ATTN_HEADS * HEAD_DIM
    n_hg = d // gw
    tq = ATTN_TILE
    nq = seq // tq
    return pl.pallas_call(
        _attn_kernel,
        grid=(batch, n_hg, nq),
        in_specs=[pl.BlockSpec((tq, gw), lambda b, h, i: (b * nq + i, 2 * n_hg + h)),
                  pl.BlockSpec((seq, gw), lambda b, h, i: (b, h)),
                  pl.BlockSpec((seq, gw), lambda b, h, i: (b, n_hg + h))],
        out_specs=pl.BlockSpec((tq, gw), lambda b, h, i: (b * nq + i, h)),
        out_shape=jax.ShapeDtypeStruct((t, d), BF16),
        scratch_shapes=[pltpu.VMEM((ATTN_HEADS, tq, HEAD_DIM), F32),
                        pltpu.VMEM((ATTN_HEADS, tq, LANES), F32),
                        pltpu.VMEM((ATTN_HEADS, tq, ATTN_KEYS), F32),
                        pltpu.VMEM((ATTN_HEADS, tq, LANES), F32)],
        compiler_params=_params("arbitrary", "arbitrary", "arbitrary"),
        name="attention",
    )(kvq, kvq, kvq)


def _oproj_route_kernel(o_ref, wo_ref, h_ref, gffn_ref, wr_ref, br_ref,
                        hout_ref, xr_ref, cls_ref, *, n_grp, n_exp):
    ts = TOKEN_TILE
    for sub in range(o_ref.shape[0] // ts):
        rows = pl.ds(sub * ts, ts)
        h = h_ref[rows, :] + jnp.dot(o_ref[rows, :], wo_ref[...], preferred_element_type=F32)
        hout_ref[rows, :] = h
        _ffn_norm_route(h, gffn_ref, wr_ref, br_ref, xr_ref, cls_ref, sub, n_grp, n_exp)


def _oproj_route(o, wo, h, gffn, wr, br, n_grp, n_exp):
    t, d = h.shape
    ts, step = TOKEN_TILE, ROUTE_STEP
    tok = lambda i: (i, 0)
    fixed2 = lambda i: (0, 0)
    return pl.pallas_call(
        functools.partial(_oproj_route_kernel, n_grp=n_grp, n_exp=n_exp),
        grid=(t // step,),
        in_specs=[pl.BlockSpec((step, d), tok),
                  pl.BlockSpec((d, d), fixed2),
                  pl.BlockSpec((step, d), tok),
                  pl.BlockSpec((1, d), fixed2),
                  pl.BlockSpec((d, LANES), fixed2),
                  pl.BlockSpec((1, LANES), fixed2)],
        out_specs=(pl.BlockSpec((step, d), tok),
                   pl.BlockSpec((step, d // 2 + LANES), tok),
                   pl.BlockSpec((step // ts, SUBLANES, ts), lambda i: (i, 0, 0))),
        out_shape=_route_out(t, d, ts),
        compiler_params=_params("arbitrary"),
        name="oproj_route",
    )(o, wo, h, gffn, wr, br)


def _router_weights(w_grp, b_grp, w_rt, b_rt):
    d, n_grp = w_grp.shape
    n_exp = w_rt.shape[2]
    w = jnp.concatenate([w_grp, jnp.transpose(w_rt, (1, 0, 2)).reshape(d, n_grp * n_exp)], axis=1)
    b = jnp.concatenate([b_grp, b_rt.reshape(-1)])
    pad = LANES - w.shape[1]
    return (jnp.pad(w, ((0, 0), (0, pad))).astype(BF16), jnp.pad(b, (0, pad)).reshape(1, LANES))


def _pair_table(n_grp, n_exp):
    lo, hi = [], []
    for g in range(n_grp):
        for a in range(n_exp):
            for b in range(a + 1, n_exp):
                lo.append(g * n_exp + a)
                hi.append(g * n_exp + b)
    return jnp.array(lo, jnp.int32), jnp.array(hi, jnp.int32)


def _moe_layer(h, xr, cls3, wg, wu, wd, first_expert, n_grp, n_exp):
    t, d = h.shape
    tm = MOE_TILE
    cls_lo, cls_hi = _pair_table(n_grp, n_exp)
    n_cls = cls_lo.shape[0]
    n_tiles = t // tm + n_cls
    assert n_cls <= CLASS_ROWS and n_tiles < 256
    pos3, tiles = _positions(cls3, tm)
    tiles_c = tiles[:n_cls, 0].astype(jnp.int32)
    ends = jnp.cumsum(tiles_c)
    n_act = ends[-1]
    tail = jnp.where(tiles_c > 0, (ends - 1) * tm, -1).astype(jnp.int32)
    tile_ids = jnp.minimum(jnp.arange(n_tiles, dtype=jnp.int32), n_act - 1)
    tile_cls = jnp.sum(tile_ids[:, None] >= ends[None, :], axis=1)
    xs = _dispatch(jnp.concatenate([tail, n_act.reshape(1)]), pos3, xr, n_tiles, tm)
    ys = _moe(first_expert + cls_lo[tile_cls], first_expert + cls_hi[tile_cls], n_act.reshape(1),
              xs, wg, wu, wd, tm)
    return _combine(pos3, h, ys)


def kernel(x, norm_mix, norm_ffn, pool_w, pool_scale, kv_norm, w_kv, k_norm, w_q, q_norm, w_o,
           moe_w_grp, moe_b_grp, moe_w_rt, moe_b_rt, moe_w_gate, moe_w_up, moe_w_down):
    batch, seq, d = x.shape
    t = batch * seq
    n_grp, n_exp = moe_w_rt.shape[1], moe_w_rt.shape[3]
    n_heads = d // HEAD_DIM
    assert len(POOL_WINDOWS) == pool_w.shape[1] and seq % ROUTE_STEP == 0 and seq % ATTN_TILE == 0
    assert t % PROJ_TILE == 0 and d % PROJ_COLS == 0 and n_grp + n_grp * n_exp <= LANES
    assert n_heads % ATTN_HEADS == 0 and seq % ATTN_KEYS == 0 and ATTN_KEYS % ATTN_TILE == 0
    row = lambda v: v.reshape(1, -1).astype(F32)

    n_all = moe_w_gate.shape[1]
    f = moe_w_gate.shape[3]
    wg = _cast_bf16([moe_w_gate.reshape(-1, f)]).reshape(-1, d, f)
    wu = _cast_bf16([moe_w_up.reshape(-1, f)]).reshape(-1, d, f)
    wd = _cast_bf16([moe_w_down.reshape(-1, d)]).reshape(-1, f, d)

    wr0, br0 = _router_weights(moe_w_grp[0], moe_b_grp[0], moe_w_rt[0], moe_b_rt[0])
    h1, xr, cls3 = _pool_route(x.reshape(t, d), row(norm_mix[0]), pool_w[0].astype(BF16),
                               row(pool_scale[0]), row(norm_ffn[0]), wr0, br0, batch, n_grp, n_exp)
    h2 = _moe_layer(h1, xr, cls3, wg, wu, wd, 0, n_grp, n_exp)

    w_all = _cast_bf16([w_kv, w_q[0]])
    q_gain = q_norm[0] * (HEAD_DIM ** -0.5 * LOG2E)
    gain_all = jnp.concatenate([jnp.tile(k_norm, n_heads), jnp.ones((d,), F32), jnp.tile(q_gain, n_heads)])
    kvq = _kvq(h2, row(kv_norm), row(norm_mix[1]), w_all, row(gain_all))
    o = _attention(kvq, batch, d)

    wr1, br1 = _router_weights(moe_w_grp[1], moe_b_grp[1], moe_w_rt[1], moe_b_rt[1])
    h3, xr, cls3 = _oproj_route(o, _cast_bf16([w_o[0]]), h2, row(norm_ffn[1]), wr1, br1, n_grp, n_exp)
    h4 = _moe_layer(h3, xr, cls3, wg, wu, wd, n_all, n_grp, n_exp)
    return h4.reshape(batch, seq, d)
```
